```python
import jax, jax.numpy as jnp
from jax import lax
import numpy as np


D_MODEL = 1024
BATCH = 4
SEQ = 4096
DEPTH = 2

D_MIX = D_MODEL
SC_W = D_MIX // 4
SC_K = 3
CF_W = D_MIX // 4
CF_K = 31
GLA_DV = D_MIX // 2
GLA_DK = GLA_DV // 2
GLA_H = 4
GLA_HDK = GLA_DK // GLA_H
GLA_HDV = GLA_DV // GLA_H
GLA_RANK = 16
GLA_TAU = 16.0
GLA_CHUNK = 64
D_FF = 2816
N_EXPERTS = 8
TOP_K = 2
N_DENSE = (DEPTH + 1) // 2
N_MOE = DEPTH // 2
EPS = 1e-6
IN_SIZES = (SC_W, SC_W, SC_W,
            CF_W, CF_W,
            GLA_DK, GLA_DK, GLA_DV,
            GLA_RANK, GLA_DV)
D_IN = sum(IN_SIZES)

kernel_name = "hybrid_conv_conformer_gla_moe_block"


def _rmsnorm(x, g):
    xf = x.astype(jnp.float32)
    y = xf * lax.rsqrt(jnp.mean(xf * xf, axis=-1, keepdims=True) + EPS)
    return (y * g.astype(jnp.float32)).astype(x.dtype)


def _layernorm(x, g, b):
    xf = x.astype(jnp.float32)
    mu = jnp.mean(xf, axis=-1, keepdims=True)
    var = jnp.mean(jnp.square(xf - mu), axis=-1, keepdims=True)
    y = (xf - mu) * lax.rsqrt(var + EPS)
    return (y * g.astype(jnp.float32) + b.astype(jnp.float32)).astype(x.dtype)


def _causal_dwconv(x, w):
    K, C = w.shape
    return lax.conv_general_dilated(
        x, w[:, None, :].astype(x.dtype), window_strides=(1,), padding=[(K - 1, 0)],
        dimension_numbers=("NWC", "WIO", "NWC"), feature_group_count=C)


def _gla(q, k, v, g):
    B, S, H, dk = q.shape
    dv = v.shape[-1]
    n = S // GLA_CHUNK

    def to_chunks(t):
        return t.astype(jnp.float32).reshape(B, n, GLA_CHUNK, H, t.shape[-1]).transpose(1, 0, 3, 2, 4)

    qc = to_chunks(q * (dk ** -0.5))
    kc, vc, gc = to_chunks(k), to_chunks(v), to_chunks(g)
    causal = jnp.tril(jnp.ones((GLA_CHUNK, GLA_CHUNK), dtype=bool))

    def step(state, inp):
        qi, ki, vi, gi = inp
        b = jnp.cumsum(gi, axis=2)
        o_inter = jnp.einsum("bhik,bhkv->bhiv", qi * jnp.exp(b), state)
        diff = b[:, :, :, None, :] - b[:, :, None, :, :]
        decay = jnp.exp(jnp.where(causal[:, :, None], diff, -jnp.inf))
        scores = jnp.einsum("bhik,bhjk,bhijk->bhij", qi, ki, decay)
        o_intra = jnp.einsum("bhij,bhjv->bhiv", scores, vi)
        b_last = b[:, :, -1:, :]
        new_state = (jnp.exp(b_last[:, :, 0, :])[..., None] * state
                     + jnp.einsum("bhjk,bhjv->bhkv", ki * jnp.exp(b_last - b), vi))
        return new_state, o_inter + o_intra

    s0 = jnp.zeros((B, H, dk, dv), jnp.float32)
    _, o = lax.scan(step, s0, (qc, kc, vc, gc))
    return o.transpose(1, 0, 3, 2, 4).reshape(B, S, H, dv)


def _mixer(h, w_in, sc_w, cf_w, cf_b, cf_g, cf_beta, w_a2, b_a, gla_g, w_out):
    B, S, _ = h.shape
    p = h @ w_in
    split_idx = [int(i) for i in np.cumsum(IN_SIZES)[:-1]]
    sc_b, sc_c, sc_v, cf_a, cf_gate, q, k, v, a_lr, r = jnp.split(p, split_idx, axis=-1)

    y_sc = sc_b * _causal_dwconv(sc_c * sc_v, sc_w)

    u = cf_a * jax.nn.sigmoid(cf_gate)
    u = _causal_dwconv(u, cf_w) + cf_b
    y_cf = jax.nn.silu(_layernorm(u, cf_g, cf_beta))

    log_a = jax.nn.log_sigmoid((a_lr @ w_a2 + b_a).astype(jnp.float32)) / GLA_TAU
    o = _gla(q.reshape(B, S, GLA_H, GLA_HDK), k.reshape(B, S, GLA_H, GLA_HDK),
             v.reshape(B, S, GLA_H, GLA_HDV), log_a.reshape(B, S, GLA_H, GLA_HDK))
    o = _rmsnorm(o, gla_g).astype(h.dtype)
    y_gla = (o * jax.nn.silu(r).reshape(B, S, GLA_H, GLA_HDV)).reshape(B, S, GLA_DV)

    return jnp.concatenate([y_sc, y_cf, y_gla], axis=-1) @ w_out


def _swiglu(t, w_g, w_u, w_d):
    return (jax.nn.silu(t @ w_g) * (t @ w_u)) @ w_d


def _moe(h, w_router, w_g, w_u, w_d):
    B, S, D = h.shape
    t = h.reshape(-1, D)
    logits = (t @ w_router).astype(jnp.float32)
    top_v, top_i = lax.top_k(logits, TOP_K)
    top_w = jax.nn.softmax(top_v, axis=-1)
    gates = jnp.sum(jax.nn.one_hot(top_i, N_EXPERTS, dtype=jnp.float32) * top_w[..., None], axis=1)
    out = jnp.zeros(t.shape, jnp.float32)
    for e in range(N_EXPERTS):
        out = out + gates[:, e:e + 1] * _swiglu(t, w_g[e], w_u[e], w_d[e]).astype(jnp.float32)
    return out.astype(h.dtype).reshape(B, S, D)


def setup_inputs(seed: int = 0) -> dict:
    key = jax.random.key(seed)
    ks = jax.random.split(key, 24)
    nrm = lambda k, shape, scale: jax.random.normal(k, shape, jnp.float32) * scale
    gain = lambda k, shape: 1.0 + 0.01 * jax.random.normal(k, shape, jnp.float32)
    return {
        "x": jax.random.normal(ks[0], (BATCH, SEQ, D_MODEL), jnp.float32),
        "attn_norm_g": gain(ks[1], (DEPTH, D_MODEL)),
        "w_in": nrm(ks[2], (DEPTH, D_MODEL, D_IN), D_MODEL ** -0.5),
        "sc_conv_w": nrm(ks[3], (DEPTH, SC_K, SC_W), SC_K ** -0.5),
        "cf_conv_w": nrm(ks[4], (DEPTH, CF_K, CF_W), CF_K ** -0.5),
        "cf_conv_b": nrm(ks[5], (DEPTH, CF_W), 0.01),
        "cf_ln_g": gain(ks[6], (DEPTH, CF_W)),
        "cf_ln_b": nrm(ks[7], (DEPTH, CF_W), 0.01),
        "gla_w_a2": nrm(ks[8], (DEPTH, GLA_RANK, GLA_DK), GLA_RANK ** -0.5),
        "gla_b_a": nrm(ks[9], (DEPTH, GLA_DK), 0.01),
        "gla_norm_g": gain(ks[10], (DEPTH, GLA_H, GLA_HDV)),
        "w_out": nrm(ks[11], (DEPTH, D_MIX, D_MODEL), D_MIX ** -0.5),
        "ffn_norm_g": gain(ks[12], (DEPTH, D_MODEL)),
        "dense_w_gate": nrm(ks[13], (N_DENSE, D_MODEL, D_FF), D_MODEL ** -0.5),
        "dense_w_up": nrm(ks[14], (N_DENSE, D_MODEL, D_FF), D_MODEL ** -0.5),
        "dense_w_down": nrm(ks[15], (N_DENSE, D_FF, D_MODEL), D_FF ** -0.5),
        "moe_w_router": nrm(ks[16], (N_MOE, D_MODEL, N_EXPERTS), D_MODEL ** -0.5),
        "moe_w_gate": nrm(ks[17], (N_MOE, N_EXPERTS, D_MODEL, D_FF), D_MODEL ** -0.5),
        "moe_w_up": nrm(ks[18], (N_MOE, N_EXPERTS, D_MODEL, D_FF), D_MODEL ** -0.5),
        "moe_w_down": nrm(ks[19], (N_MOE, N_EXPERTS, D_FF, D_MODEL), D_FF ** -0.5),
        "final_norm_g": gain(ks[20], (D_MODEL,)),
    }


def reference(x, attn_norm_g, w_in, sc_conv_w, cf_conv_w, cf_conv_b, cf_ln_g, cf_ln_b,
              gla_w_a2, gla_b_a, gla_norm_g, w_out, ffn_norm_g,
              dense_w_gate, dense_w_up, dense_w_down,
              moe_w_router, moe_w_gate, moe_w_up, moe_w_down, final_norm_g):
    h = x
    for layer in range(DEPTH):
        a = _rmsnorm(h, attn_norm_g[layer])
        h = h + _mixer(a, w_in[layer], sc_conv_w[layer], cf_conv_w[layer], cf_conv_b[layer],
                       cf_ln_g[layer], cf_ln_b[layer], gla_w_a2[layer], gla_b_a[layer],
                       gla_norm_g[layer], w_out[layer])
        f = _rmsnorm(h, ffn_norm_g[layer])
        j = layer // 2
        if layer % 2 == 0:
            h = h + _swiglu(f, dense_w_gate[j], dense_w_up[j], dense_w_down[j])
        else:
            h = h + _moe(f, moe_w_router[j], moe_w_gate[j], moe_w_up[j], moe_w_down[j])
    return _rmsnorm(h, final_norm_g)
```

```python
import functools

import numpy as np
import jax
import jax.numpy as jnp
from jax import lax
from jax.experimental import pallas as pl
from jax.experimental.pallas import tpu as pltpu

F32 = jnp.float32
BF16 = jnp.bfloat16

EPS = 1e-6
SC_K = 3
CF_K = 31
GLA_H = 4
GLA_TAU = 16.0
CHUNK = 64
N_EXPERTS = 8
LANES = 128
VMEM_LIMIT = 56 * 1024 * 1024

W = 256
C_SCB, C_SCC, C_SCV, C_CFA, C_CFG, C_Q, C_K = (i * W for i in range(7))
C_V = 7 * W
C_R = C_V + 2 * W
C_ALR = C_R + 2 * W
D_INP = C_ALR + LANES


def _split_hi_lo(x):
    hi = x.astype(BF16)
    lo = (x - hi.astype(F32)).astype(BF16)
    return hi, lo


def _dot(a, b):
    return jnp.dot(a, b, preferred_element_type=F32)


def _dot_nt(a, b):
    return lax.dot_general(a, b, (((1,), (1,)), ((), ())), preferred_element_type=F32)


def _dot_tn(a, b):
    return lax.dot_general(a, b, (((0,), (0,)), ((), ())), preferred_element_type=F32)


def _level_matrices():
    mats = [np.tril(np.ones((CHUNK, CHUNK), np.float32))]
    s = CHUNK // 2
    while s >= 1:
        m = np.zeros((CHUNK, CHUNK), np.float32)
        for i in range(CHUNK):
            mid = (i // (2 * s)) * 2 * s + s
            if i >= mid:
                m[i, mid:i + 1] = 1.0
            else:
                m[i, i + 1:mid] = 1.0
        mats.append(m)
        s //= 2
    return np.concatenate(mats, axis=0)


N_LEVELS = int(np.log2(CHUNK))


def _gla_chunk(c, p_scr, y_scr, st_scr, wa2_hi, wa2_lo, ba, gng, mall):
    dk = W // GLA_H
    dv = 2 * W // GLA_H
    r0 = pl.multiple_of(c * CHUNK, CHUNK)
    rows = pl.ds(r0, CHUNK)
    q = p_scr[rows, C_Q:C_Q + W] * (dk ** -0.5)
    k = p_scr[rows, C_K:C_K + W]
    v = p_scr[rows, C_V:C_V + 2 * W].astype(BF16)
    r = p_scr[rows, C_R:C_R + 2 * W]
    alr_hi, alr_lo = _split_hi_lo(p_scr[rows, C_ALR:C_ALR + LANES])
    z = _dot(alr_hi, wa2_hi) + _dot(alr_lo, wa2_hi) + _dot(alr_hi, wa2_lo) + ba
    g = (jnp.minimum(z, 0.0) - jnp.log1p(jnp.exp(-jnp.abs(z)))) * (1.0 / GLA_TAU)
    g_hi, g_lo = _split_hi_lo(g)
    ex = _dot(mall, g_hi) + _dot(mall, g_lo)
    b = ex[0:CHUNK]
    b_last = b[CHUNK - 1:CHUNK]

    dk_sh = dk.bit_length() - 1
    dv_sh = dv.bit_length() - 1
    row = lax.broadcasted_iota(jnp.int32, (CHUNK, W), 0)
    lane_head = lax.broadcasted_iota(jnp.int32, (CHUNK, W), 1) >> dk_sh
    srow = lax.broadcasted_iota(jnp.int32, (GLA_H * CHUNK, CHUNK), 0) & (CHUNK - 1)
    scol = lax.broadcasted_iota(jnp.int32, (GLA_H * CHUNK, CHUNK), 1)

    def stack_heads(x):
        xb = x.astype(BF16)
        zero = jnp.zeros_like(xb)
        return jnp.concatenate([jnp.where(lane_head == h, xb, zero) for h in range(GLA_H)], axis=0)

    scores = jnp.where(srow == scol, _dot_nt(stack_heads(q), k.astype(BF16)), 0.0)
    for lvl in range(1, N_LEVELS + 1):
        s = CHUNK >> lvl
        sh = s.bit_length() - 1
        e = jnp.exp(ex[lvl * CHUNK:(lvl + 1) * CHUNK])
        second = ((row >> sh) & 1) == 1
        ql = jnp.where(second, q * e, 0.0)
        kl = jnp.where(second, 0.0, k * e)
        same = (srow >> (sh + 1)) == (scol >> (sh + 1))
        scores = scores + jnp.where(same, _dot_nt(stack_heads(ql), kl.astype(BF16)), 0.0)

    vhead = lax.broadcasted_iota(jnp.int32, (CHUNK, 2 * W), 1) >> dv_sh
    o_stack = _dot(scores.astype(BF16), v)
    q_in = (q * jnp.exp(b)).astype(BF16)
    st = st_scr[...]
    o = _dot_nt(q_in, st.astype(BF16))
    for h in range(GLA_H):
        o = o + jnp.where(vhead == h, o_stack[h * CHUNK:(h + 1) * CHUNK], 0.0)

    k_s = (k * jnp.exp(b_last - b)).astype(BF16)
    upd = _dot_tn(v, k_s)
    rhead = lax.broadcasted_iota(jnp.int32, (2 * W, W), 0) >> dv_sh
    chead = lax.broadcasted_iota(jnp.int32, (2 * W, W), 1) >> dk_sh
    st_scr[...] = st * jnp.exp(b_last) + jnp.where(rhead == chead, upd, 0.0)

    for h in range(GLA_H):
        oh = o[:, h * dv:(h + 1) * dv]
        ms = jnp.mean(oh * oh, axis=-1, keepdims=True)
        rh = r[:, h * dv:(h + 1) * dv]
        yh = oh * lax.rsqrt(ms + EPS) * gng[:, h * dv:(h + 1) * dv] * (rh * jax.nn.sigmoid(rh))
        y_scr[rows, 2 * W + h * dv:2 * W + (h + 1) * dv] = yh.astype(BF16)


def _mixer_kernel(x_ref, ng_ref, win_ref, scw_ref, cfw_ref, cfb_ref, lng_ref, lnb_ref,
                  wa2_ref, ba_ref, gng_ref, mall_ref, wout_ref, o_ref,
                  p_scr, sc_buf, cf_buf, st_scr, y_scr, *, ts):
    sc_halo = 8
    cf_halo = 32

    @pl.when(pl.program_id(1) == 0)
    def _():
        sc_buf[0:sc_halo, :] = jnp.zeros((sc_halo, W), F32)
        cf_buf[0:cf_halo, :] = jnp.zeros((cf_halo, W), F32)
        st_scr[...] = jnp.zeros(st_scr.shape, F32)

    x = x_ref[0]
    ms = jnp.mean(x * x, axis=-1, keepdims=True)
    a = (x * lax.rsqrt(ms + EPS) * ng_ref[...]).astype(BF16)
    p_scr[...] = _dot(a, win_ref[...])

    sc_buf[sc_halo:sc_halo + ts, :] = p_scr[:, C_SCC:C_SCC + W] * p_scr[:, C_SCV:C_SCV + W]
    cf_buf[cf_halo:cf_halo + ts, :] = p_scr[:, C_CFA:C_CFA + W] * jax.nn.sigmoid(p_scr[:, C_CFG:C_CFG + W])
    scw = scw_ref[...]
    cfw = cfw_ref[...]
    for c in range(ts // CHUNK):
        r0 = c * CHUNK
        conv = jnp.zeros((CHUNK, W), F32)
        for j in range(SC_K):
            o0 = r0 + sc_halo - (SC_K - 1) + j
            conv = conv + sc_buf[o0:o0 + CHUNK, :] * scw[j:j + 1]
        y_scr[r0:r0 + CHUNK, 0:W] = (p_scr[r0:r0 + CHUNK, C_SCB:C_SCB + W] * conv).astype(BF16)

        u = jnp.zeros((CHUNK, W), F32) + cfb_ref[...]
        for j in range(CF_K):
            o0 = r0 + cf_halo - (CF_K - 1) + j
            u = u + cf_buf[o0:o0 + CHUNK, :] * cfw[j:j + 1]
        mu = jnp.mean(u, axis=-1, keepdims=True)
        d = u - mu
        var = jnp.mean(d * d, axis=-1, keepdims=True)
        yn = d * lax.rsqrt(var + EPS) * lng_ref[...] + lnb_ref[...]
        y_scr[r0:r0 + CHUNK, W:2 * W] = (yn * jax.nn.sigmoid(yn)).astype(BF16)
    sc_buf[0:sc_halo, :] = sc_buf[ts:ts + sc_halo, :]
    cf_buf[0:cf_halo, :] = cf_buf[ts:ts + cf_halo, :]

    wa2_hi, wa2_lo = _split_hi_lo(wa2_ref[...])
    ba = ba_ref[...]
    gng = gng_ref[...]
    mall = mall_ref[...]

    def body(c, carry):
        _gla_chunk(c, p_scr, y_scr, st_scr, wa2_hi, wa2_lo, ba, gng, mall)
        return carry

    lax.fori_loop(0, ts // CHUNK, body, 0)

    o_ref[0] = x + _dot(y_scr[...], wout_ref[...])


def _const_spec(shape):
    nd = len(shape)
    return pl.BlockSpec(shape, lambda *_: (0,) * nd)


def _mixer(x, ng, w_in, sc_w, cf_w, cf_b, ln_g, ln_b, w_a2, b_a, gn_g, w_out, *, ts):
    B, S, D = x.shape
    ts = min(ts, S)
    split = np.cumsum([0, W, W, W, W, W, W, W, 2 * W, 16, 2 * W])
    cols = [w_in[:, split[i]:split[i + 1]] for i in range(10)]
    w_alr = jnp.pad(cols[8], ((0, 0), (0, LANES - 16)))
    w_in_r = jnp.concatenate(cols[:8] + [cols[9], w_alr], axis=1).astype(BF16)
    scw = jnp.pad(sc_w, ((0, 8 - SC_K), (0, 0)))
    cfw = jnp.pad(cf_w, ((0, 32 - CF_K), (0, 0)))
    wa2 = jnp.pad(w_a2, ((0, LANES - 16), (0, 0)))
    mall = jnp.asarray(_level_matrices(), BF16)
    consts = [ng.reshape(1, D), w_in_r, scw, cfw, cf_b.reshape(1, W), ln_g.reshape(1, W),
              ln_b.reshape(1, W), wa2, b_a.reshape(1, W), gn_g.reshape(1, 2 * W), mall,
              w_out.astype(BF16)]
    return pl.pallas_call(
        functools.partial(_mixer_kernel, ts=ts),
        grid=(B, S // ts),
        in_specs=[pl.BlockSpec((1, ts, D), lambda b, t: (b, t, 0))] + [_const_spec(c.shape) for c in consts],
        out_specs=pl.BlockSpec((1, ts, D), lambda b, t: (b, t, 0)),
        out_shape=jax.ShapeDtypeStruct((B, S, D), F32),
        scratch_shapes=[
            pltpu.VMEM((ts, D_INP), F32),
            pltpu.VMEM((ts + 8, W), F32),
            pltpu.VMEM((ts + 32, W), F32),
            pltpu.VMEM((2 * W, W), F32),
            pltpu.VMEM((ts, D), BF16),
        ],
        compiler_params=pltpu.CompilerParams(
            dimension_semantics=("arbitrary", "arbitrary"), vmem_limit_bytes=VMEM_LIMIT),
        name="mixer",
    )(x, *consts)


def _ff_chunks(d_ff, step=1024):
    return [(s, min(s + step, d_ff)) for s in range(0, d_ff, step)]


def _swiglu_rows(f, wg_ref, wu_ref, wd_ref, lead=()):
    d_ff = wg_ref.shape[-1]
    acc = None
    for s, e in _ff_chunks(d_ff):
        gate = _dot(f, wg_ref[lead + (slice(None), slice(s, e))])
        up = _dot(f, wu_ref[lead + (slice(None), slice(s, e))])
        act = (gate * jax.nn.sigmoid(gate) * up).astype(BF16)
        part = _dot(act, wd_ref[lead + (slice(s, e), slice(None))])
        acc = part if acc is None else acc + part
    return acc


def _ffn_kernel(h_ref, ng_ref, wg_ref, wu_ref, wd_ref, o_ref):
    h = h_ref[...]
    ms = jnp.mean(h * h, axis=-1, keepdims=True)
    f = (h * lax.rsqrt(ms + EPS) * ng_ref[...]).astype(BF16)
    o_ref[...] = h + _swiglu_rows(f, wg_ref, wu_ref, wd_ref)


def _dense_ffn(h, ng, w_g, w_u, w_d, *, tm):
    T, D = h.shape
    tm = min(tm, T)
    consts = [ng.reshape(1, D), w_g.astype(BF16), w_u.astype(BF16), w_d.astype(BF16)]
    return pl.pallas_call(
        _ffn_kernel,
        grid=(T // tm,),
        in_specs=[pl.BlockSpec((tm, D), lambda i: (i, 0))] + [_const_spec(c.shape) for c in consts],
        out_specs=pl.BlockSpec((tm, D), lambda i: (i, 0)),
        out_shape=jax.ShapeDtypeStruct((T, D), F32),
        compiler_params=pltpu.CompilerParams(
            dimension_semantics=("arbitrary",), vmem_limit_bytes=VMEM_LIMIT),
        name="dense_ffn",
    )(h, *consts)


def _router_kernel(h_ref, ng_ref, wr_ref, f_ref, meta_ref, cnt_ref, carry):
    tb = h_ref.shape[0]

    @pl.when(pl.program_id(0) == 0)
    def _():
        carry[...] = jnp.zeros(carry.shape, F32)

    h = h_ref[...]
    ms = jnp.mean(h * h, axis=-1, keepdims=True)
    f = h * lax.rsqrt(ms + EPS) * ng_ref[...]
    f_ref[...] = f.astype(BF16)
    f_hi, f_lo = _split_hi_lo(f)
    w_hi, w_lo = _split_hi_lo(wr_ref[...])
    logits = _dot(f_hi, w_hi) + _dot(f_lo, w_hi) + _dot(f_hi, w_lo)
    lane = lax.broadcasted_iota(jnp.int32, (tb, LANES), 1).astype(F32)
    neg = jnp.float32(-jnp.inf)
    logits = jnp.where(lane < N_EXPERTS, logits, neg)
    m1 = jnp.max(logits, axis=-1, keepdims=True)
    i1 = jnp.min(jnp.where(logits == m1, lane, float(LANES)), axis=-1, keepdims=True)
    rest = jnp.where(lane == i1, neg, logits)
    m2 = jnp.max(rest, axis=-1, keepdims=True)
    i2 = jnp.min(jnp.where(rest == m2, lane, float(LANES)), axis=-1, keepdims=True)
    e21 = jnp.exp(m2 - m1)
    g1 = 1.0 / (1.0 + e21)
    g2 = e21 * g1

    sel1 = lane == i1
    sel2 = lane == i2
    sel = jnp.where(sel1 | sel2, 1.0, 0.0)
    tri = lax.broadcasted_iota(jnp.int32, (tb, tb), 0) > lax.broadcasted_iota(jnp.int32, (tb, tb), 1)
    before = _dot(jnp.where(tri, 1.0, 0.0).astype(BF16), sel.astype(BF16)) + carry[...]
    r1 = jnp.sum(jnp.where(sel1, before, 0.0), axis=-1, keepdims=True)
    r2 = jnp.sum(jnp.where(sel2, before, 0.0), axis=-1, keepdims=True)
    blk_cnt = jnp.sum(sel, axis=0, keepdims=True)
    carry[...] = carry[...] + blk_cnt
    cnt_ref[0] = jnp.broadcast_to(blk_cnt, (8, LANES))

    vals = [i1, i2, r1, r2, g1, g2]
    meta = jnp.zeros((tb, LANES), F32)
    for j, val in enumerate(vals):
        meta = jnp.where(lane == j, val, meta)
    meta_ref[...] = meta


def _router(h, ng, w_router, *, tb):
    T, D = h.shape
    nblk = T // tb
    wr = jnp.pad(w_router, ((0, 0), (0, LANES - N_EXPERTS)))
    return pl.pallas_call(
        _router_kernel,
        grid=(nblk,),
        in_specs=[pl.BlockSpec((tb, D), lambda i: (i, 0)), _const_spec((1, D)), _const_spec(wr.shape)],
        out_specs=[pl.BlockSpec((tb, D), lambda i: (i, 0)),
                   pl.BlockSpec((tb, LANES), lambda i: (i, 0)),
                   pl.BlockSpec((1, 8, LANES), lambda i: (i, 0, 0))],
        out_shape=[jax.ShapeDtypeStruct((T, D), BF16),
                   jax.ShapeDtypeStruct((T, LANES), F32),
                   jax.ShapeDtypeStruct((nblk, 8, LANES), F32)],
        scratch_shapes=[pltpu.VMEM((1, LANES), F32)],
        compiler_params=pltpu.CompilerParams(
            dimension_semantics=("arbitrary",), vmem_limit_bytes=VMEM_LIMIT),
        name="router",
    )(h, ng.reshape(1, D), wr)


def _dispatch_kernel(tile_s, blk_s, first_s, valid_s, f_ref, slots_ref, gates_ref, xs_ref, ws_ref):
    i = pl.program_id(0)
    tm = xs_ref.shape[0]
    tb = f_ref.shape[0]

    @pl.when(valid_s[i] == 1)
    def _():
        slot_id = tile_s[i] * tm + lax.broadcasted_iota(jnp.int32, (tm, tb), 0)
        hit1 = slot_id == slots_ref[0, 0:1, :]
        hit2 = slot_id == slots_ref[0, 1:2, :]
        onehot = jnp.where(hit1 | hit2, 1.0, 0.0).astype(BF16)
        rows = _dot(onehot, f_ref[...])
        gate = jnp.sum(jnp.where(hit1, gates_ref[0, 0:1, :], 0.0) + jnp.where(hit2, gates_ref[0, 1:2, :], 0.0),
                       axis=-1, keepdims=True)

        @pl.when(first_s[i] == 1)
        def _():
            xs_ref[...] = rows.astype(BF16)
            ws_ref[...] = gate

        @pl.when(first_s[i] == 0)
        def _():
            xs_ref[...] = (xs_ref[...].astype(F32) + rows).astype(BF16)
            ws_ref[...] = ws_ref[...] + gate


def _dispatch(f, slots_l, gates_l, pair_tile, pair_blk, pair_first, pair_valid, *, n_tiles, tm, tb):
    T, D = f.shape
    n_pairs = pair_tile.shape[0]
    grid_spec = pltpu.PrefetchScalarGridSpec(
        num_scalar_prefetch=4,
        grid=(n_pairs,),
        in_specs=[pl.BlockSpec((tb, D), lambda i, ts_, bs, fs, vs: (bs[i], 0)),
                  pl.BlockSpec((1, 8, tb), lambda i, ts_, bs, fs, vs: (bs[i], 0, 0)),
                  pl.BlockSpec((1, 8, tb), lambda i, ts_, bs, fs, vs: (bs[i], 0, 0))],
        out_specs=[pl.BlockSpec((tm, D), lambda i, ts_, bs, fs, vs: (ts_[i], 0)),
                   pl.BlockSpec((tm, 1), lambda i, ts_, bs, fs, vs: (ts_[i], 0))],
    )
    return pl.pallas_call(
        _dispatch_kernel,
        grid_spec=grid_spec,
        out_shape=[jax.ShapeDtypeStruct((n_tiles * tm, D), BF16),
                   jax.ShapeDtypeStruct((n_tiles * tm, 1), F32)],
        compiler_params=pltpu.CompilerParams(
            dimension_semantics=("arbitrary",), vmem_limit_bytes=VMEM_LIMIT),
        name="dispatch",
    )(pair_tile, pair_blk, pair_first, pair_valid, f, slots_l, gates_l)


def _expert_kernel(te_s, xs_ref, ws_ref, wg_ref, wu_ref, wd_ref, ys_ref):
    y = _swiglu_rows(xs_ref[...], wg_ref, wu_ref, wd_ref, lead=(0,))
    ys_ref[...] = (y * ws_ref[...]).astype(BF16)


def _experts(xs, ws, tile_expert, w_g, w_u, w_d, *, tm):
    n_slots, D = xs.shape
    d_ff = w_g.shape[-1]
    grid_spec = pltpu.PrefetchScalarGridSpec(
        num_scalar_prefetch=1,
        grid=(n_slots // tm,),
        in_specs=[pl.BlockSpec((tm, D), lambda i, te: (i, 0)),
                  pl.BlockSpec((tm, 1), lambda i, te: (i, 0)),
                  pl.BlockSpec((1, D, d_ff), lambda i, te: (te[i], 0, 0)),
                  pl.BlockSpec((1, D, d_ff), lambda i, te: (te[i], 0, 0)),
                  pl.BlockSpec((1, d_ff, D), lambda i, te: (te[i], 0, 0))],
        out_specs=pl.BlockSpec((tm, D), lambda i, te: (i, 0)),
    )
    return pl.pallas_call(
        _expert_kernel,
        grid_spec=grid_spec,
        out_shape=jax.ShapeDtypeStruct((n_slots, D), BF16),
        compiler_params=pltpu.CompilerParams(
            dimension_semantics=("arbitrary",), vmem_limit_bytes=VMEM_LIMIT),
        name="experts",
    )(tile_expert, xs, ws, w_g, w_u, w_d)


def _combine_kernel(tile_s, blk_s, first_s, last_s, valid_s, ys_ref, slots_ref, h_ref, ng_ref, o_ref, acc):
    i = pl.program_id(0)
    tm = ys_ref.shape[0]
    tb = h_ref.shape[0]

    @pl.when(first_s[i] == 1)
    def _():
        acc[...] = h_ref[...]

    @pl.when(valid_s[i] == 1)
    def _():
        slot_id = tile_s[i] * tm + lax.broadcasted_iota(jnp.int32, (tb, tm), 1)
        hit = (slot_id == slots_ref[:, 0:1]) | (slot_id == slots_ref[:, 1:2])
        acc[...] += _dot(jnp.where(hit, 1.0, 0.0).astype(BF16), ys_ref[...])

    @pl.when(last_s[i] == 1)
    def _():
        hh = acc[...]
        ms = jnp.mean(hh * hh, axis=-1, keepdims=True)
        o_ref[...] = hh * lax.rsqrt(ms + EPS) * ng_ref[...]


def _combine(ys, slots_c, h, ng, pair_tile, pair_blk, pair_first, pair_last, pair_valid, *, tm, tb):
    T, D = h.shape
    n_pairs = pair_tile.shape[0]
    idx_t = lambda i, ts_, bs, fs, ls, vs: (ts_[i], 0)
    idx_b = lambda i, ts_, bs, fs, ls, vs: (bs[i], 0)
    grid_spec = pltpu.PrefetchScalarGridSpec(
        num_scalar_prefetch=5,
        grid=(n_pairs,),
        in_specs=[pl.BlockSpec((tm, D), idx_t),
                  pl.BlockSpec((tb, 2), idx_b),
                  pl.BlockSpec((tb, D), idx_b),
                  pl.BlockSpec((1, D), lambda i, *_: (0, 0))],
        out_specs=pl.BlockSpec((tb, D), idx_b),
        scratch_shapes=[pltpu.VMEM((tb, D), F32)],
    )
    return pl.pallas_call(
        _combine_kernel,
        grid_spec=grid_spec,
        out_shape=jax.ShapeDtypeStruct((T, D), F32),
        compiler_params=pltpu.CompilerParams(
            dimension_semantics=("arbitrary",), vmem_limit_bytes=VMEM_LIMIT),
        name="combine",
    )(pair_tile, pair_blk, pair_first, pair_last, pair_valid, ys, slots_c, h, ng.reshape(1, D))


def _enumerate_pairs(first_tile, n_tile, blk_id, n_pairs):
    ends = jnp.cumsum(n_tile)
    total = ends[-1]
    i = jnp.arange(n_pairs, dtype=jnp.int32)
    ic = jnp.minimum(i, total - 1)
    g = jnp.searchsorted(ends, ic, side="right").astype(jnp.int32)
    tile = first_tile[g] + (ic - (ends[g] - n_tile[g]))
    blk = blk_id[g]
    valid = (i < total).astype(jnp.int32)
    return tile.astype(jnp.int32), blk.astype(jnp.int32), valid


def _moe_and_final_norm(h, ng, w_router, w_g, w_u, w_d, final_g, *, tm, tb):
    T, D = h.shape
    tb = min(tb, T)
    tm = min(tm, tb)
    nblk = T // tb
    E = N_EXPERTS
    n_tiles = (2 * T) // tm + E
    f, meta, blk_cnt = _router(h, ng, w_router, tb=tb)

    e12 = meta[:, 0:2].astype(jnp.int32)
    r12 = meta[:, 2:4].astype(jnp.int32)
    g12 = meta[:, 4:6]
    cnt = blk_cnt[:, 0, :E].astype(jnp.int32)
    tot = jnp.sum(cnt, axis=0)
    padded = ((tot + tm - 1) // tm) * tm
    off = jnp.cumsum(padded) - padded
    slots = off[e12] + r12
    tile_expert = jnp.minimum(
        jnp.searchsorted(jnp.cumsum(padded), jnp.arange(n_tiles, dtype=jnp.int32) * tm, side="right"),
        E - 1).astype(jnp.int32)
    n_used = jnp.sum(padded) // tm

    start = off[None, :] + jnp.cumsum(cnt, axis=0) - cnt
    t_first = start // tm
    t_cnt = jnp.where(cnt > 0, (start + cnt - 1) // tm - t_first + 1, 0)
    blk_ids = jnp.broadcast_to(jnp.arange(nblk, dtype=jnp.int32)[:, None], (nblk, E))

    n_pairs = n_tiles + E * nblk
    d_first = jnp.concatenate([t_first.T.reshape(-1), n_used[None]])
    d_cnt = jnp.concatenate([t_cnt.T.reshape(-1), (n_tiles - n_used)[None]])
    d_blk = jnp.concatenate([blk_ids.T.reshape(-1), jnp.zeros((1,), jnp.int32)])
    d_tile, d_blkid, d_valid = _enumerate_pairs(d_first, d_cnt, d_blk, n_pairs)
    d_firstflag = jnp.concatenate([jnp.ones((1,), jnp.int32), (d_tile[1:] != d_tile[:-1]).astype(jnp.int32)])

    slots_l = jnp.pad(slots.reshape(nblk, tb, 2).transpose(0, 2, 1), ((0, 0), (0, 6), (0, 0)), constant_values=-1)
    gates_l = jnp.pad(g12.reshape(nblk, tb, 2).transpose(0, 2, 1), ((0, 0), (0, 6), (0, 0)))
    xs, ws = _dispatch(f, slots_l, gates_l, d_tile, d_blkid, d_firstflag, d_valid,
                       n_tiles=n_tiles, tm=tm, tb=tb)

    ys = _experts(xs, ws, tile_expert, w_g.astype(BF16), w_u.astype(BF16), w_d.astype(BF16), tm=tm)

    c_pairs = n_tiles + E * nblk
    c_tile, c_blkid, c_valid = _enumerate_pairs(t_first.reshape(-1), t_cnt.reshape(-1), blk_ids.reshape(-1), c_pairs)
    chg = (c_blkid[1:] != c_blkid[:-1]).astype(jnp.int32)
    c_first = jnp.concatenate([jnp.ones((1,), jnp.int32), chg])
    c_last = jnp.concatenate([chg, jnp.ones((1,), jnp.int32)])
    return _combine(ys, slots, h, final_g, c_tile, c_blkid, c_first, c_last, c_valid, tm=tm, tb=tb)


def kernel(x, attn_norm_g, w_in, sc_conv_w, cf_conv_w, cf_conv_b, cf_ln_g, cf_ln_b, gla_w_a2, gla_b_a,
           gla_norm_g, w_out, ffn_norm_g, dense_w_gate, dense_w_up, dense_w_down, moe_w_router,
           moe_w_gate, moe_w_up, moe_w_down, final_norm_g):
    B, S, D = x.shape
    depth = w_in.shape[0]
    assert depth == 2, "layer schedule below is dense FFN then routed experts"
    h = x
    for layer in range(depth):
        h = _mixer(h, attn_norm_g[layer], w_in[layer], sc_conv_w[layer], cf_conv_w[layer], cf_conv_b[layer],
                   cf_ln_g[layer], cf_ln_b[layer], gla_w_a2[layer], gla_b_a[layer], gla_norm_g[layer],
                   w_out[layer], ts=256)
        h2 = h.reshape(B * S, D)
        if layer % 2 == 0:
            h = _dense_ffn(h2, ffn_norm_g[layer], dense_w_gate[0], dense_w_up[0], dense_w_down[0],
                           tm=512).reshape(B, S, D)
        else:
            h = _moe_and_final_norm(h2, ffn_norm_g[layer], moe_w_router[0], moe_w_gate[0], moe_w_up[0],
                                    moe_w_down[0], final_norm_g, tm=512, tb=512).reshape(B, S, D)
    return h
```

```python
import functools

import numpy as np
import jax
import jax.numpy as jnp
from jax import lax
from jax.experimental import pallas as pl
from jax.experimental.pallas import tpu as pltpu

F32 = jnp.float32
BF16 = jnp.bfloat16

EPS = 1e-6
SC_K = 3
CF_K = 31
GLA_H = 4
GLA_TAU = 16.0
CHUNK = 64
N_EXPERTS = 8
LANES = 128
ROW_SUB = 8
VMEM_LIMIT = 56 * 1024 * 1024

W = 256
C_SCB, C_SCC, C_SCV, C_CFA, C_CFG, C_Q, C_K = (i * W for i in range(7))
C_V = 7 * W
C_R = C_V + 2 * W
C_ALR = C_R + 2 * W
D_INP = C_ALR + LANES


def _split_hi_lo(x):
    hi = x.astype(BF16)
    lo = (x - hi.astype(F32)).astype(BF16)
    return hi, lo


def _dot(a, b):
    return jnp.dot(a, b, preferred_element_type=F32)


def _dot_nt(a, b):
    return lax.dot_general(a, b, (((1,), (1,)), ((), ())), preferred_element_type=F32)


def _dot_tn(a, b):
    return lax.dot_general(a, b, (((0,), (0,)), ((), ())), preferred_element_type=F32)


def _level_matrices():
    mats = [np.tril(np.ones((CHUNK, CHUNK), np.float32))]
    s = CHUNK // 2
    while s >= 1:
        m = np.zeros((CHUNK, CHUNK), np.float32)
        for i in range(CHUNK):
            mid = (i // (2 * s)) * 2 * s + s
            if i >= mid:
                m[i, mid:i + 1] = 1.0
            else:
                m[i, i + 1:mid] = 1.0
        mats.append(m)
        s //= 2
    return np.concatenate(mats, axis=0)


N_LEVELS = int(np.log2(CHUNK))


def _gla_chunk(c, p_scr, y_scr, st_scr, wa2_hi, wa2_lo, ba, gng, mall):
    dk = W // GLA_H
    dv = 2 * W // GLA_H
    r0 = pl.multiple_of(c * CHUNK, CHUNK)
    rows = pl.ds(r0, CHUNK)
    q = p_scr[rows, C_Q:C_Q + W] * (dk ** -0.5)
    k = p_scr[rows, C_K:C_K + W]
    v = p_scr[rows, C_V:C_V + 2 * W].astype(BF16)
    r = p_scr[rows, C_R:C_R + 2 * W]
    alr_hi, alr_lo = _split_hi_lo(p_scr[rows, C_ALR:C_ALR + LANES])
    z = _dot(alr_hi, wa2_hi) + _dot(alr_lo, wa2_hi) + _dot(alr_hi, wa2_lo) + ba
    g = (jnp.minimum(z, 0.0) - jnp.log1p(jnp.exp(-jnp.abs(z)))) * (1.0 / GLA_TAU)
    g_hi, g_lo = _split_hi_lo(g)
    ex = _dot(mall, g_hi) + _dot(mall, g_lo)
    b = ex[0:CHUNK]
    b_last = b[CHUNK - 1:CHUNK]

    dk_sh = dk.bit_length() - 1
    dv_sh = dv.bit_length() - 1
    row = lax.broadcasted_iota(jnp.int32, (CHUNK, W), 0)
    lane_head = lax.broadcasted_iota(jnp.int32, (CHUNK, W), 1) >> dk_sh
    srow = lax.broadcasted_iota(jnp.int32, (GLA_H * CHUNK, CHUNK), 0) & (CHUNK - 1)
    scol = lax.broadcasted_iota(jnp.int32, (GLA_H * CHUNK, CHUNK), 1)

    def stack_heads(x):
        xb = x.astype(BF16)
        zero = jnp.zeros_like(xb)
        return jnp.concatenate([jnp.where(lane_head == h, xb, zero) for h in range(GLA_H)], axis=0)

    scores = jnp.where(srow == scol, _dot_nt(stack_heads(q), k.astype(BF16)), 0.0)
    for lvl in range(1, N_LEVELS + 1):
        s = CHUNK >> lvl
        sh = s.bit_length() - 1
        e = jnp.exp(ex[lvl * CHUNK:(lvl + 1) * CHUNK])
        second = ((row >> sh) & 1) == 1
        ql = jnp.where(second, q * e, 0.0)
        kl = jnp.where(second, 0.0, k * e)
        same = (srow >> (sh + 1)) == (scol >> (sh + 1))
        scores = scores + jnp.where(same, _dot_nt(stack_heads(ql), kl.astype(BF16)), 0.0)

    vhead = lax.broadcasted_iota(jnp.int32, (CHUNK, 2 * W), 1) >> dv_sh
    o_stack = _dot(scores.astype(BF16), v)
    q_in = (q * jnp.exp(b)).astype(BF16)
    st = st_scr[...]
    o = _dot_nt(q_in, st.astype(BF16))
    for h in range(GLA_H):
        o = o + jnp.where(vhead == h, o_stack[h * CHUNK:(h + 1) * CHUNK], 0.0)

    k_s = (k * jnp.exp(b_last - b)).astype(BF16)
    upd = _dot_tn(v, k_s)
    rhead = lax.broadcasted_iota(jnp.int32, (2 * W, W), 0) >> dv_sh
    chead = lax.broadcasted_iota(jnp.int32, (2 * W, W), 1) >> dk_sh
    st_scr[...] = st * jnp.exp(b_last) + jnp.where(rhead == chead, upd, 0.0)

    for h in range(GLA_H):
        oh = o[:, h * dv:(h + 1) * dv]
        ms = jnp.mean(oh * oh, axis=-1, keepdims=True)
        rh = r[:, h * dv:(h + 1) * dv]
        yh = oh * lax.rsqrt(ms + EPS) * gng[:, h * dv:(h + 1) * dv] * (rh * jax.nn.sigmoid(rh))
        y_scr[rows, 2 * W + h * dv:2 * W + (h + 1) * dv] = yh.astype(BF16)


def _mixer_kernel(x_ref, ng_ref, win_ref, scw_ref, cfw_ref, cfb_ref, lng_ref, lnb_ref,
                  wa2_ref, ba_ref, gng_ref, mall_ref, wout_ref, o_ref,
                  p_scr, sc_buf, cf_buf, st_scr, y_scr, *, ts):
    sc_halo = 8
    cf_halo = 32

    @pl.when(pl.program_id(1) == 0)
    def _():
        sc_buf[0:sc_halo, :] = jnp.zeros((sc_halo, W), F32)
        cf_buf[0:cf_halo, :] = jnp.zeros((cf_halo, W), F32)
        st_scr[...] = jnp.zeros(st_scr.shape, F32)

    x = x_ref[0]
    ms = jnp.mean(x * x, axis=-1, keepdims=True)
    a = (x * lax.rsqrt(ms + EPS) * ng_ref[...]).astype(BF16)
    p_scr[...] = _dot(a, win_ref[...])

    sc_buf[sc_halo:sc_halo + ts, :] = p_scr[:, C_SCC:C_SCC + W] * p_scr[:, C_SCV:C_SCV + W]
    cf_buf[cf_halo:cf_halo + ts, :] = p_scr[:, C_CFA:C_CFA + W] * jax.nn.sigmoid(p_scr[:, C_CFG:C_CFG + W])
    scw = scw_ref[...]
    cfw = cfw_ref[...]
    for c in range(ts // CHUNK):
        r0 = c * CHUNK
        conv = jnp.zeros((CHUNK, W), F32)
        for j in range(SC_K):
            o0 = r0 + sc_halo - (SC_K - 1) + j
            conv = conv + sc_buf[o0:o0 + CHUNK, :] * scw[j:j + 1]
        y_scr[r0:r0 + CHUNK, 0:W] = (p_scr[r0:r0 + CHUNK, C_SCB:C_SCB + W] * conv).astype(BF16)

        u = jnp.zeros((CHUNK, W), F32) + cfb_ref[...]
        for j in range(CF_K):
            o0 = r0 + cf_halo - (CF_K - 1) + j
            u = u + cf_buf[o0:o0 + CHUNK, :] * cfw[j:j + 1]
        mu = jnp.mean(u, axis=-1, keepdims=True)
        d = u - mu
        var = jnp.mean(d * d, axis=-1, keepdims=True)
        yn = d * lax.rsqrt(var + EPS) * lng_ref[...] + lnb_ref[...]
        y_scr[r0:r0 + CHUNK, W:2 * W] = (yn * jax.nn.sigmoid(yn)).astype(BF16)
    sc_buf[0:sc_halo, :] = sc_buf[ts:ts + sc_halo, :]
    cf_buf[0:cf_halo, :] = cf_buf[ts:ts + cf_halo, :]

    wa2_hi, wa2_lo = _split_hi_lo(wa2_ref[...])
    ba = ba_ref[...]
    gng = gng_ref[...]
    mall = mall_ref[...]

    def body(c, carry):
        _gla_chunk(c, p_scr, y_scr, st_scr, wa2_hi, wa2_lo, ba, gng, mall)
        return carry

    lax.fori_loop(0, ts // CHUNK, body, 0)

    o_ref[0] = x + _dot(y_scr[...], wout_ref[...])


def _const_spec(shape):
    nd = len(shape)
    return pl.BlockSpec(shape, lambda *_: (0,) * nd)


def _mixer(x, ng, w_in, sc_w, cf_w, cf_b, ln_g, ln_b, w_a2, b_a, gn_g, w_out, *, ts):
    B, S, D = x.shape
    ts = min(ts, S)
    split = np.cumsum([0, W, W, W, W, W, W, W, 2 * W, 16, 2 * W])
    cols = [w_in[:, split[i]:split[i + 1]] for i in range(10)]
    w_alr = jnp.pad(cols[8], ((0, 0), (0, LANES - 16)))
    w_in_r = jnp.concatenate(cols[:8] + [cols[9], w_alr], axis=1).astype(BF16)
    scw = jnp.pad(sc_w, ((0, 8 - SC_K), (0, 0)))
    cfw = jnp.pad(cf_w, ((0, 32 - CF_K), (0, 0)))
    wa2 = jnp.pad(w_a2, ((0, LANES - 16), (0, 0)))
    mall = jnp.asarray(_level_matrices(), BF16)
    consts = [ng.reshape(1, D), w_in_r, scw, cfw, cf_b.reshape(1, W), ln_g.reshape(1, W),
              ln_b.reshape(1, W), wa2, b_a.reshape(1, W), gn_g.reshape(1, 2 * W), mall,
              w_out.astype(BF16)]
    return pl.pallas_call(
        functools.partial(_mixer_kernel, ts=ts),
        grid=(B, S // ts),
        in_specs=[pl.BlockSpec((1, ts, D), lambda b, t: (b, t, 0))] + [_const_spec(c.shape) for c in consts],
        out_specs=pl.BlockSpec((1, ts, D), lambda b, t: (b, t, 0)),
        out_shape=jax.ShapeDtypeStruct((B, S, D), F32),
        scratch_shapes=[
            pltpu.VMEM((ts, D_INP), F32),
            pltpu.VMEM((ts + 8, W), F32),
            pltpu.VMEM((ts + 32, W), F32),
            pltpu.VMEM((2 * W, W), F32),
            pltpu.VMEM((ts, D), BF16),
        ],
        compiler_params=pltpu.CompilerParams(
            dimension_semantics=("arbitrary", "arbitrary"), vmem_limit_bytes=VMEM_LIMIT),
        name="mixer",
    )(x, *consts)


def _ff_chunks(d_ff, step=1024):
    return [(s, min(s + step, d_ff)) for s in range(0, d_ff, step)]


def _swiglu_rows(f, wg_ref, wu_ref, wd_ref, lead=()):
    d_ff = wg_ref.shape[-1]
    acc = None
    for s, e in _ff_chunks(d_ff):
        gate = _dot(f, wg_ref[lead + (slice(None), slice(s, e))])
        up = _dot(f, wu_ref[lead + (slice(None), slice(s, e))])
        act = (gate * jax.nn.sigmoid(gate) * up).astype(BF16)
        part = _dot(act, wd_ref[lead + (slice(s, e), slice(None))])
        acc = part if acc is None else acc + part
    return acc


def _ffn_kernel(h_ref, ng_ref, wg_ref, wu_ref, wd_ref, o_ref):
    h = h_ref[...]
    ms = jnp.mean(h * h, axis=-1, keepdims=True)
    f = (h * lax.rsqrt(ms + EPS) * ng_ref[...]).astype(BF16)
    o_ref[...] = h + _swiglu_rows(f, wg_ref, wu_ref, wd_ref)


def _dense_ffn(h, ng, w_g, w_u, w_d, *, tm):
    T, D = h.shape
    tm = min(tm, T)
    consts = [ng.reshape(1, D), w_g.astype(BF16), w_u.astype(BF16), w_d.astype(BF16)]
    return pl.pallas_call(
        _ffn_kernel,
        grid=(T // tm,),
        in_specs=[pl.BlockSpec((tm, D), lambda i: (i, 0))] + [_const_spec(c.shape) for c in consts],
        out_specs=pl.BlockSpec((tm, D), lambda i: (i, 0)),
        out_shape=jax.ShapeDtypeStruct((T, D), F32),
        compiler_params=pltpu.CompilerParams(
            dimension_semantics=("arbitrary",), vmem_limit_bytes=VMEM_LIMIT),
        name="dense_ffn",
    )(h, *consts)


def _router_kernel(h_ref, ng_ref, wr_ref, f_ref, meta_ref, cnt_ref, carry):
    tb = h_ref.shape[0]

    @pl.when(pl.program_id(0) == 0)
    def _():
        carry[...] = jnp.zeros(carry.shape, F32)

    h = h_ref[...]
    ms = jnp.mean(h * h, axis=-1, keepdims=True)
    f = h * lax.rsqrt(ms + EPS) * ng_ref[...]
    for c in range(ROW_SUB):
        f_ref[:, c, :] = f[:, c * LANES:(c + 1) * LANES]
    f_hi, f_lo = _split_hi_lo(f)
    w_hi, w_lo = _split_hi_lo(wr_ref[...])
    logits = _dot(f_hi, w_hi) + _dot(f_lo, w_hi) + _dot(f_hi, w_lo)
    lane = lax.broadcasted_iota(jnp.int32, (tb, LANES), 1).astype(F32)
    neg = jnp.float32(-jnp.inf)
    logits = jnp.where(lane < N_EXPERTS, logits, neg)
    m1 = jnp.max(logits, axis=-1, keepdims=True)
    i1 = jnp.min(jnp.where(logits == m1, lane, float(LANES)), axis=-1, keepdims=True)
    rest = jnp.where(lane == i1, neg, logits)
    m2 = jnp.max(rest, axis=-1, keepdims=True)
    i2 = jnp.min(jnp.where(rest == m2, lane, float(LANES)), axis=-1, keepdims=True)
    e21 = jnp.exp(m2 - m1)
    g1 = 1.0 / (1.0 + e21)
    g2 = e21 * g1

    sel1 = lane == i1
    sel2 = lane == i2
    sel = jnp.where(sel1 | sel2, 1.0, 0.0)
    tri = lax.broadcasted_iota(jnp.int32, (tb, tb), 0) > lax.broadcasted_iota(jnp.int32, (tb, tb), 1)
    before = _dot(jnp.where(tri, 1.0, 0.0).astype(BF16), sel.astype(BF16)) + carry[...]
    r1 = jnp.sum(jnp.where(sel1, before, 0.0), axis=-1, keepdims=True)
    r2 = jnp.sum(jnp.where(sel2, before, 0.0), axis=-1, keepdims=True)
    blk_cnt = jnp.sum(sel, axis=0, keepdims=True)
    carry[...] = carry[...] + blk_cnt
    cnt_ref[0] = jnp.broadcast_to(blk_cnt, (8, LANES))

    vals = [i1, i2, r1, r2, g1, g2]
    meta = jnp.zeros((tb, LANES), F32)
    for j, val in enumerate(vals):
        meta = jnp.where(lane == j, val, meta)
    meta_ref[...] = meta


def _router(h, ng, w_router, *, tb):
    T, D = h.shape
    nblk = T // tb
    wr = jnp.pad(w_router, ((0, 0), (0, LANES - N_EXPERTS)))
    return pl.pallas_call(
        _router_kernel,
        grid=(nblk,),
        in_specs=[pl.BlockSpec((tb, D), lambda i: (i, 0)), _const_spec((1, D)), _const_spec(wr.shape)],
        out_specs=[pl.BlockSpec((tb, ROW_SUB, LANES), lambda i: (i, 0, 0)),
                   pl.BlockSpec((tb, LANES), lambda i: (i, 0)),
                   pl.BlockSpec((1, 8, LANES), lambda i: (i, 0, 0))],
        out_shape=[jax.ShapeDtypeStruct((T, ROW_SUB, LANES), F32),
                   jax.ShapeDtypeStruct((T, LANES), F32),
                   jax.ShapeDtypeStruct((nblk, 8, LANES), F32)],
        scratch_shapes=[pltpu.VMEM((1, LANES), F32)],
        compiler_params=pltpu.CompilerParams(
            dimension_semantics=("arbitrary",), vmem_limit_bytes=VMEM_LIMIT),
        name="router",
    )(h, ng.reshape(1, D), wr)


def _expert_kernel(te_s, nused_s, tos_s, dst_s, f_hbm, ws_ref, wg_ref, wu_ref, wd_ref, out_hbm,
                   xbuf, ybuf, sem_g, sem_s):
    i = pl.program_id(0)
    n_used = nused_s[0]
    tm = xbuf.shape[1]
    slot = i % 2

    def gather_copy(hbm_row, buf, r):
        return pltpu.make_async_copy(f_hbm.at[hbm_row], xbuf.at[buf, r], sem_g.at[buf])

    def scatter_copy(hbm_row, buf, r):
        return pltpu.make_async_copy(ybuf.at[buf, r], out_hbm.at[hbm_row], sem_s.at[buf])

    def start_all(make, idx_s, tile, buf):
        for r in range(tm):
            make(idx_s[tile * tm + r], buf, r).start()

    def wait_all(make, buf):
        for r in range(tm):
            make(0, buf, r).wait()

    @pl.when(i == 0)
    def _():
        start_all(gather_copy, tos_s, 0, 0)
        ybuf[...] = jnp.zeros(ybuf.shape, F32)
        n_real = out_hbm.shape[0] - 2 * tm
        fills = [pltpu.make_async_copy(ybuf.at[b], out_hbm.at[pl.ds(n_real + b * tm, tm)], sem_s.at[b])
                 for b in range(2)]
        for cp in fills:
            cp.start()
        for cp in fills:
            cp.wait()

    @pl.when(i < n_used)
    def _():
        wait_all(gather_copy, slot)

        start_all(gather_copy, tos_s, jnp.minimum(i + 1, pl.num_programs(0) - 1), 1 - slot)

        x = jnp.concatenate([xbuf[slot, :, c, :] for c in range(ROW_SUB)], axis=-1).astype(BF16)
        y = _swiglu_rows(x, wg_ref, wu_ref, wd_ref, lead=(0,)) * ws_ref[...]

        @pl.when(i >= 2)
        def _():
            wait_all(scatter_copy, slot)

        for c in range(ROW_SUB):
            ybuf[slot, :, c, :] = y[:, c * LANES:(c + 1) * LANES]
        start_all(scatter_copy, dst_s, i, slot)

        @pl.when(i == n_used - 1)
        def _():
            wait_all(gather_copy, 1 - slot)
            wait_all(scatter_copy, slot)

            @pl.when(n_used >= 2)
            def _():
                wait_all(scatter_copy, 1 - slot)


def _experts(f3, ws, tile_expert, n_used, tos, dst, w_g, w_u, w_d, *, tm, n_out):
    n_slots = ws.shape[0]
    D = w_g.shape[1]
    d_ff = w_g.shape[-1]
    grid_spec = pltpu.PrefetchScalarGridSpec(
        num_scalar_prefetch=4,
        grid=(n_slots // tm,),
        in_specs=[pl.BlockSpec(memory_space=pl.ANY),
                  pl.BlockSpec((tm, 1), lambda i, te, *_: (i, 0)),
                  pl.BlockSpec((1, D, d_ff), lambda i, te, *_: (te[i], 0, 0)),
                  pl.BlockSpec((1, D, d_ff), lambda i, te, *_: (te[i], 0, 0)),
                  pl.BlockSpec((1, d_ff, D), lambda i, te, *_: (te[i], 0, 0))],
        out_specs=pl.BlockSpec(memory_space=pl.ANY),
        scratch_shapes=[pltpu.VMEM((2, tm, ROW_SUB, LANES), F32),
                        pltpu.VMEM((2, tm, ROW_SUB, LANES), F32),
                        pltpu.SemaphoreType.DMA((2,)),
                        pltpu.SemaphoreType.DMA((2,))],
    )
    return pl.pallas_call(
        _expert_kernel,
        grid_spec=grid_spec,
        out_shape=jax.ShapeDtypeStruct((n_out, ROW_SUB, LANES), F32),
        compiler_params=pltpu.CompilerParams(
            dimension_semantics=("arbitrary",), vmem_limit_bytes=VMEM_LIMIT),
        name="experts",
    )(tile_expert, n_used, tos, dst, f3, ws, w_g, w_u, w_d)


def _final_kernel(h_ref, y1_ref, y2_ref, ng_ref, o_ref):
    moe = jnp.concatenate([y1_ref[:, c, :] + y2_ref[:, c, :] for c in range(ROW_SUB)], axis=-1)
    hh = h_ref[...] + moe
    ms = jnp.mean(hh * hh, axis=-1, keepdims=True)
    o_ref[...] = hh * lax.rsqrt(ms + EPS) * ng_ref[...]


def _final(h, y3, ng, *, tb):
    T, D = h.shape
    nblk = T // tb
    return pl.pallas_call(
        _final_kernel,
        grid=(nblk,),
        in_specs=[pl.BlockSpec((tb, D), lambda i: (i, 0)),
                  pl.BlockSpec((tb, ROW_SUB, LANES), lambda i: (i, 0, 0)),
                  pl.BlockSpec((tb, ROW_SUB, LANES), lambda i: (i + nblk, 0, 0)),
                  _const_spec((1, D))],
        out_specs=pl.BlockSpec((tb, D), lambda i: (i, 0)),
        out_shape=jax.ShapeDtypeStruct((T, D), F32),
        compiler_params=pltpu.CompilerParams(
            dimension_semantics=("arbitrary",), vmem_limit_bytes=VMEM_LIMIT),
        name="final",
    )(h, y3, y3, ng.reshape(1, D))


def _moe_and_final_norm(h, ng, w_router, w_g, w_u, w_d, final_g, *, tm, tb):
    T, D = h.shape
    tb = min(tb, T)
    tm = min(tm, tb)
    nblk = T // tb
    E = N_EXPERTS
    assert D == ROW_SUB * LANES and T % tb == 0
    n_tiles = (2 * T) // tm + E
    n_slots = n_tiles * tm
    f3, meta, blk_cnt = _router(h, ng, w_router, tb=tb)

    e12 = meta[:, 0:2].astype(jnp.int32)
    r12 = meta[:, 2:4].astype(jnp.int32)
    tot = jnp.sum(blk_cnt[:, 0, :E], axis=0).astype(jnp.int32)
    padded = ((tot + tm - 1) // tm) * tm
    ends = jnp.cumsum(padded)
    off = ends - padded
    sel = e12[:, :, None] == jnp.arange(E, dtype=jnp.int32)
    slots = jnp.sum(jnp.where(sel, off, 0), axis=-1) + r12
    tile_start = jnp.arange(n_tiles, dtype=jnp.int32) * tm
    tile_expert = jnp.minimum(jnp.sum(ends[None, :] <= tile_start[:, None], axis=1), E - 1).astype(jnp.int32)
    n_used = (ends[-1] // tm).reshape(1).astype(jnp.int32)

    tok = jnp.arange(T, dtype=jnp.int32)
    pos = jnp.arange(n_slots, dtype=jnp.int32)
    scratch_row = 2 * T + ((pos // tm) % 2) * tm + pos % tm
    base = jnp.stack([jnp.zeros((n_slots,), F32), scratch_row.astype(F32), jnp.zeros((n_slots,), F32)], axis=1)
    dest = jnp.stack([tok, tok + T], axis=1).astype(F32)
    payload = jnp.stack([jnp.stack([tok, tok], axis=1).astype(F32), dest, meta[:, 4:6]], axis=2).reshape(2 * T, 3)
    table = base.at[slots.reshape(-1)].set(payload, unique_indices=True)
    tos = table[:, 0].astype(jnp.int32)
    dst = table[:, 1].astype(jnp.int32)
    ws = table[:, 2:3]

    y3 = _experts(f3, ws, tile_expert, n_used, tos, dst, w_g.astype(BF16), w_u.astype(BF16), w_d.astype(BF16),
                  tm=tm, n_out=2 * T + 2 * tm)
    return _final(h, y3, final_g, tb=tb)


def kernel(x, attn_norm_g, w_in, sc_conv_w, cf_conv_w, cf_conv_b, cf_ln_g, cf_ln_b, gla_w_a2, gla_b_a,
           gla_norm_g, w_out, ffn_norm_g, dense_w_gate, dense_w_up, dense_w_down, moe_w_router,
           moe_w_gate, moe_w_up, moe_w_down, final_norm_g):
    B, S, D = x.shape
    depth = w_in.shape[0]
    assert depth == 2, "layer schedule below is dense FFN then routed experts"
    h = x
    for layer in range(depth):
        h = _mixer(h, attn_norm_g[layer], w_in[layer], sc_conv_w[layer], cf_conv_w[layer], cf_conv_b[layer],
                   cf_ln_g[layer], cf_ln_b[layer], gla_w_a2[layer], gla_b_a[layer], gla_norm_g[layer],
                   w_out[layer], ts=256)
        h2 = h.reshape(B * S, D)
        if layer % 2 == 0:
            h = _dense_ffn(h2, ffn_norm_g[layer], dense_w_gate[0], dense_w_up[0], dense_w_down[0],
                           tm=512).reshape(B, S, D)
        else:
            h = _moe_and_final_norm(h2, ffn_norm_g[layer], moe_w_router[0], moe_w_gate[0], moe_w_up[0],
                                    moe_w_down[0], final_norm_g, tm=512, tb=512).reshape(B, S, D)
    return h
```

```python
import functools

import numpy as np
import jax
import jax.numpy as jnp
from jax import lax
from jax.experimental import pallas as pl
from jax.experimental.pallas import tpu as pltpu

F32 = jnp.float32
BF16 = jnp.bfloat16

EPS = 1e-6
SC_K = 3
CF_K = 31
GLA_H = 4
GLA_TAU = 16.0
CHUNK = 64
GATE_STEP_BOUND = 1.0
N_EXPERTS = 8
LANES = 128
ROW_SUB = 8
VMEM_LIMIT = 56 * 1024 * 1024

W = 256
C_SCB, C_SCC, C_SCV, C_CFA, C_CFG, C_Q, C_K = (i * W for i in range(7))
C_V = 7 * W
C_R = C_V + 2 * W
C_ALR = C_R + 2 * W
D_INP = C_ALR + LANES


def _split_hi_lo(x):
    hi = x.astype(BF16)
    lo = (x - hi.astype(F32)).astype(BF16)
    return hi, lo


def _dot(a, b):
    return jnp.dot(a, b, preferred_element_type=F32)


def _dot_nt(a, b):
    return lax.dot_general(a, b, (((1,), (1,)), ((), ())), preferred_element_type=F32)


def _dot_tn(a, b):
    return lax.dot_general(a, b, (((0,), (0,)), ((), ())), preferred_element_type=F32)


def _level_matrices():
    mats = [np.tril(np.ones((CHUNK, CHUNK), np.float32))]
    s = CHUNK // 2
    while s >= 1:
        m = np.zeros((CHUNK, CHUNK), np.float32)
        for i in range(CHUNK):
            mid = (i // (2 * s)) * 2 * s + s
            if i >= mid:
                m[i, mid:i + 1] = 1.0
            else:
                m[i, i + 1:mid] = 1.0
        mats.append(m)
        s //= 2
    return np.concatenate(mats, axis=0)


N_LEVELS = int(np.log2(CHUNK))


def _gla_chunk(c, p_scr, g_scr, y_scr, st_scr, gng, mall, *, bounded):
    dk = W // GLA_H
    dv = 2 * W // GLA_H
    if isinstance(c, int):
        rows = slice(c * CHUNK, (c + 1) * CHUNK)
    else:
        rows = pl.ds(pl.multiple_of(c * CHUNK, CHUNK), CHUNK)
    q = p_scr[rows, C_Q:C_Q + W] * (dk ** -0.5)
    k = p_scr[rows, C_K:C_K + W]
    v = p_scr[rows, C_V:C_V + 2 * W].astype(BF16)
    r = p_scr[rows, C_R:C_R + 2 * W]
    g_hi, g_lo = _split_hi_lo(g_scr[rows, :])

    dk_sh = dk.bit_length() - 1
    dv_sh = dv.bit_length() - 1
    row = lax.broadcasted_iota(jnp.int32, (CHUNK, W), 0)
    lane_head = lax.broadcasted_iota(jnp.int32, (CHUNK, W), 1) >> dk_sh
    srow = lax.broadcasted_iota(jnp.int32, (GLA_H * CHUNK, CHUNK), 0) & (CHUNK - 1)
    scol = lax.broadcasted_iota(jnp.int32, (GLA_H * CHUNK, CHUNK), 1)

    def stack_heads(x):
        xb = x.astype(BF16)
        zero = jnp.zeros_like(xb)
        return jnp.concatenate([jnp.where(lane_head == h, xb, zero) for h in range(GLA_H)], axis=0)

    if bounded:
        cum = mall[0:CHUNK]
        b = _dot(cum, g_hi) + _dot(cum, g_lo)
        ql = q * jnp.exp(b)
        kl = k * jnp.exp(-b)
        scores = jnp.where(srow >= scol, _dot_nt(stack_heads(ql), kl.astype(BF16)), 0.0)
    else:
        ex = _dot(mall, g_hi) + _dot(mall, g_lo)
        b = ex[0:CHUNK]
        scores = jnp.where(srow == scol, _dot_nt(stack_heads(q), k.astype(BF16)), 0.0)
        for lvl in range(1, N_LEVELS + 1):
            s = CHUNK >> lvl
            sh = s.bit_length() - 1
            e = jnp.exp(ex[lvl * CHUNK:(lvl + 1) * CHUNK])
            second = ((row >> sh) & 1) == 1
            ql = jnp.where(second, q * e, 0.0)
            kl = jnp.where(second, 0.0, k * e)
            same = (srow >> (sh + 1)) == (scol >> (sh + 1))
            scores = scores + jnp.where(same, _dot_nt(stack_heads(ql), kl.astype(BF16)), 0.0)
    b_last = b[CHUNK - 1:CHUNK]

    vhead = lax.broadcasted_iota(jnp.int32, (CHUNK, 2 * W), 1) >> dv_sh
    o_stack = _dot(scores.astype(BF16), v)
    q_in = (q * jnp.exp(b)).astype(BF16)
    st = st_scr[...]
    o = _dot_nt(q_in, st.astype(BF16))
    for h in range(GLA_H):
        o = o + jnp.where(vhead == h, o_stack[h * CHUNK:(h + 1) * CHUNK], 0.0)

    k_s = (k * jnp.exp(b_last - b)).astype(BF16)
    upd = _dot_tn(v, k_s)
    rhead = lax.broadcasted_iota(jnp.int32, (2 * W, W), 0) >> dv_sh
    chead = lax.broadcasted_iota(jnp.int32, (2 * W, W), 1) >> dk_sh
    st_scr[...] = st * jnp.exp(b_last) + jnp.where(rhead == chead, upd, 0.0)

    for h in range(GLA_H):
        oh = o[:, h * dv:(h + 1) * dv]
        ms = jnp.mean(oh * oh, axis=-1, keepdims=True)
        rh = r[:, h * dv:(h + 1) * dv]
        yh = oh * lax.rsqrt(ms + EPS) * gng[:, h * dv:(h + 1) * dv] * (rh * jax.nn.sigmoid(rh))
        y_scr[rows, 2 * W + h * dv:2 * W + (h + 1) * dv] = yh.astype(BF16)


def _mixer_kernel(x_ref, ng_ref, win_ref, scw_ref, cfw_ref, cfb_ref, lng_ref, lnb_ref,
                  wa2_ref, ba_ref, gng_ref, mall_ref, wout_ref, o_ref,
                  p_scr, sc_buf, cf_buf, st_scr, y_scr, g_scr, *, ts):
    sc_halo = 8
    cf_halo = 32

    @pl.when(pl.program_id(1) == 0)
    def _():
        sc_buf[0:sc_halo, :] = jnp.zeros((sc_halo, W), F32)
        cf_buf[0:cf_halo, :] = jnp.zeros((cf_halo, W), F32)
        st_scr[...] = jnp.zeros(st_scr.shape, F32)

    x = x_ref[0]
    ms = jnp.mean(x * x, axis=-1, keepdims=True)
    a = (x * lax.rsqrt(ms + EPS) * ng_ref[...]).astype(BF16)
    p_scr[...] = _dot(a, win_ref[...])

    sc_buf[sc_halo:sc_halo + ts, :] = p_scr[:, C_SCC:C_SCC + W] * p_scr[:, C_SCV:C_SCV + W]
    cf_buf[cf_halo:cf_halo + ts, :] = p_scr[:, C_CFA:C_CFA + W] * jax.nn.sigmoid(p_scr[:, C_CFG:C_CFG + W])
    scw = scw_ref[...]
    cfw = cfw_ref[...]
    for c in range(ts // CHUNK):
        r0 = c * CHUNK
        conv = jnp.zeros((CHUNK, W), F32)
        for j in range(SC_K):
            o0 = r0 + sc_halo - (SC_K - 1) + j
            conv = conv + sc_buf[o0:o0 + CHUNK, :] * scw[j:j + 1]
        y_scr[r0:r0 + CHUNK, 0:W] = (p_scr[r0:r0 + CHUNK, C_SCB:C_SCB + W] * conv).astype(BF16)

        u = jnp.zeros((CHUNK, W), F32) + cfb_ref[...]
        tap0 = cf_halo - (CF_K - 1)
        for rr in range(ROW_SUB):
            n_rows = CHUNK if rr == 0 else CHUNK + ROW_SUB
            part = None
            for j in range(CF_K):
                if (tap0 + j) % ROW_SUB != rr:
                    continue
                a0 = r0 + (tap0 + j) // ROW_SUB * ROW_SUB
                term = cf_buf[a0:a0 + n_rows, :] * cfw[j:j + 1]
                part = term if part is None else part + term
            u = u + part[rr:rr + CHUNK]
        mu = jnp.mean(u, axis=-1, keepdims=True)
        d = u - mu
        var = jnp.mean(d * d, axis=-1, keepdims=True)
        yn = d * lax.rsqrt(var + EPS) * lng_ref[...] + lnb_ref[...]
        y_scr[r0:r0 + CHUNK, W:2 * W] = (yn * jax.nn.sigmoid(yn)).astype(BF16)
    sc_buf[0:sc_halo, :] = sc_buf[ts:ts + sc_halo, :]
    cf_buf[0:cf_halo, :] = cf_buf[ts:ts + cf_halo, :]

    wa2_hi, wa2_lo = _split_hi_lo(wa2_ref[...])
    alr_hi, alr_lo = _split_hi_lo(p_scr[:, C_ALR:C_ALR + LANES])
    z = _dot(alr_hi, wa2_hi) + _dot(alr_lo, wa2_hi) + _dot(alr_hi, wa2_lo) + ba_ref[...]
    g_scr[...] = (jnp.minimum(z, 0.0) - jnp.log1p(jnp.exp(-jnp.abs(z)))) * (1.0 / GLA_TAU)
    gng = gng_ref[...]
    mall = mall_ref[...]
    bounded = jnp.min(g_scr[...]) >= -GATE_STEP_BOUND

    @pl.when(bounded)
    def _():
        for c in range(ts // CHUNK):
            _gla_chunk(c, p_scr, g_scr, y_scr, st_scr, gng, mall, bounded=True)

    @pl.when(jnp.logical_not(bounded))
    def _():
        def body(c, carry):
            _gla_chunk(c, p_scr, g_scr, y_scr, st_scr, gng, mall, bounded=False)
            return carry
        lax.fori_loop(0, ts // CHUNK, body, 0)

    o_ref[0] = x + _dot(y_scr[...], wout_ref[...])


def _const_spec(shape):
    nd = len(shape)
    return pl.BlockSpec(shape, lambda *_: (0,) * nd)


def _mixer(x, ng, w_in, sc_w, cf_w, cf_b, ln_g, ln_b, w_a2, b_a, gn_g, w_out, *, ts):
    B, S, D = x.shape
    ts = min(ts, S)
    split = np.cumsum([0, W, W, W, W, W, W, W, 2 * W, 16, 2 * W])
    cols = [w_in[:, split[i]:split[i + 1]] for i in range(10)]
    w_alr = jnp.pad(cols[8], ((0, 0), (0, LANES - 16)))
    w_in_r = jnp.concatenate(cols[:8] + [cols[9], w_alr], axis=1).astype(BF16)
    scw = jnp.pad(sc_w, ((0, 8 - SC_K), (0, 0)))
    cfw = jnp.pad(cf_w, ((0, 32 - CF_K), (0, 0)))
    wa2 = jnp.pad(w_a2, ((0, LANES - 16), (0, 0)))
    mall = jnp.asarray(_level_matrices(), BF16)
    consts = [ng.reshape(1, D), w_in_r, scw, cfw, cf_b.reshape(1, W), ln_g.reshape(1, W),
              ln_b.reshape(1, W), wa2, b_a.reshape(1, W), gn_g.reshape(1, 2 * W), mall,
              w_out.astype(BF16)]
    return pl.pallas_call(
        functools.partial(_mixer_kernel, ts=ts),
        grid=(B, S // ts),
        in_specs=[pl.BlockSpec((1, ts, D), lambda b, t: (b, t, 0))] + [_const_spec(c.shape) for c in consts],
        out_specs=pl.BlockSpec((1, ts, D), lambda b, t: (b, t, 0)),
        out_shape=jax.ShapeDtypeStruct((B, S, D), F32),
        scratch_shapes=[
            pltpu.VMEM((ts, D_INP), F32),
            pltpu.VMEM((ts + 8, W), F32),
            pltpu.VMEM((ts + 32, W), F32),
            pltpu.VMEM((2 * W, W), F32),
            pltpu.VMEM((ts, D), BF16),
            pltpu.VMEM((ts, W), F32),
        ],
        compiler_params=pltpu.CompilerParams(
            dimension_semantics=("arbitrary", "arbitrary"), vmem_limit_bytes=VMEM_LIMIT),
        name="mixer",
    )(x, *consts)


def _ff_chunks(d_ff, step=1024):
    return [(s, min(s + step, d_ff)) for s in range(0, d_ff, step)]


def _swiglu_rows(f, wg_ref, wu_ref, wd_ref, lead=()):
    d_ff = wg_ref.shape[-1]
    acc = None
    for s, e in _ff_chunks(d_ff):
        gate = _dot(f, wg_ref[lead + (slice(None), slice(s, e))])
        up = _dot(f, wu_ref[lead + (slice(None), slice(s, e))])
        act = (gate * jax.nn.sigmoid(gate) * up).astype(BF16)
        part = _dot(act, wd_ref[lead + (slice(s, e), slice(None))])
        acc = part if acc is None else acc + part
    return acc


def _ffn_kernel(h_ref, ng_ref, wg_ref, wu_ref, wd_ref, o_ref):
    h = h_ref[...]
    ms = jnp.mean(h * h, axis=-1, keepdims=True)
    f = (h * lax.rsqrt(ms + EPS) * ng_ref[...]).astype(BF16)
    o_ref[...] = h + _swiglu_rows(f, wg_ref, wu_ref, wd_ref)


def _dense_ffn(h, ng, w_g, w_u, w_d, *, tm):
    T, D = h.shape
    tm = min(tm, T)
    consts = [ng.reshape(1, D), w_g.astype(BF16), w_u.astype(BF16), w_d.astype(BF16)]
    return pl.pallas_call(
        _ffn_kernel,
        grid=(T // tm,),
        in_specs=[pl.BlockSpec((tm, D), lambda i: (i, 0))] + [_const_spec(c.shape) for c in consts],
        out_specs=pl.BlockSpec((tm, D), lambda i: (i, 0)),
        out_shape=jax.ShapeDtypeStruct((T, D), F32),
        compiler_params=pltpu.CompilerParams(
            dimension_semantics=("arbitrary",), vmem_limit_bytes=VMEM_LIMIT),
        name="dense_ffn",
    )(h, *consts)


def _router_kernel(h_ref, ng_ref, wr_ref, f_ref, meta_ref, cnt_ref, carry):
    tb = h_ref.shape[0]

    @pl.when(pl.program_id(0) == 0)
    def _():
        carry[...] = jnp.zeros(carry.shape, F32)

    h = h_ref[...]
    ms = jnp.mean(h * h, axis=-1, keepdims=True)
    f = h * lax.rsqrt(ms + EPS) * ng_ref[...]
    for c in range(ROW_SUB):
        f_ref[:, c, :] = f[:, c * LANES:(c + 1) * LANES]
    f_hi, f_lo = _split_hi_lo(f)
    w_hi, w_lo = _split_hi_lo(wr_ref[...])
    logits = _dot(f_hi, w_hi) + _dot(f_lo, w_hi) + _dot(f_hi, w_lo)
    lane = lax.broadcasted_iota(jnp.int32, (tb, LANES), 1).astype(F32)
    neg = jnp.float32(-jnp.inf)
    logits = jnp.where(lane < N_EXPERTS, logits, neg)
    m1 = jnp.max(logits, axis=-1, keepdims=True)
    i1 = jnp.min(jnp.where(logits == m1, lane, float(LANES)), axis=-1, keepdims=True)
    rest = jnp.where(lane == i1, neg, logits)
    m2 = jnp.max(rest, axis=-1, keepdims=True)
    i2 = jnp.min(jnp.where(rest == m2, lane, float(LANES)), axis=-1, keepdims=True)
    e21 = jnp.exp(m2 - m1)
    g1 = 1.0 / (1.0 + e21)
    g2 = e21 * g1

    sel1 = lane == i1
    sel2 = lane == i2
    sel = jnp.where(sel1 | sel2, 1.0, 0.0)
    tri = lax.broadcasted_iota(jnp.int32, (tb, tb), 0) > lax.broadcasted_iota(jnp.int32, (tb, tb), 1)
    before = _dot(jnp.where(tri, 1.0, 0.0).astype(BF16), sel.astype(BF16)) + carry[...]
    r1 = jnp.sum(jnp.where(sel1, before, 0.0), axis=-1, keepdims=True)
    r2 = jnp.sum(jnp.where(sel2, before, 0.0), axis=-1, keepdims=True)
    blk_cnt = jnp.sum(sel, axis=0, keepdims=True)
    carry[...] = carry[...] + blk_cnt
    cnt_ref[0] = jnp.broadcast_to(blk_cnt, (8, LANES))

    vals = [i1, i2, r1, r2, g1, g2]
    meta = jnp.zeros((tb, LANES), F32)
    for j, val in enumerate(vals):
        meta = jnp.where(lane == j, val, meta)
    meta_ref[...] = meta


def _router(h, ng, w_router, *, tb):
    T, D = h.shape
    nblk = T // tb
    wr = jnp.pad(w_router, ((0, 0), (0, LANES - N_EXPERTS)))
    return pl.pallas_call(
        _router_kernel,
        grid=(nblk,),
        in_specs=[pl.BlockSpec((tb, D), lambda i: (i, 0)), _const_spec((1, D)), _const_spec(wr.shape)],
        out_specs=[pl.BlockSpec((tb, ROW_SUB, LANES), lambda i: (i, 0, 0)),
                   pl.BlockSpec((tb, LANES), lambda i: (i, 0)),
                   pl.BlockSpec((1, 8, LANES), lambda i: (i, 0, 0))],
        out_shape=[jax.ShapeDtypeStruct((T, ROW_SUB, LANES), F32),
                   jax.ShapeDtypeStruct((T, LANES), F32),
                   jax.ShapeDtypeStruct((nblk, 8, LANES), F32)],
        scratch_shapes=[pltpu.VMEM((1, LANES), F32)],
        compiler_params=pltpu.CompilerParams(
            dimension_semantics=("arbitrary",), vmem_limit_bytes=VMEM_LIMIT),
        name="router",
    )(h, ng.reshape(1, D), wr)


def _expert_kernel(te_s, nused_s, tos_s, dst_s, f_hbm, ws_ref, wg_ref, wu_ref, wd_ref, out_hbm,
                   xbuf, ybuf, sem_g, sem_s):
    i = pl.program_id(0)
    n_used = nused_s[0]
    tm = xbuf.shape[1]
    slot = i % 2

    def gather_copy(hbm_row, buf, r):
        return pltpu.make_async_copy(f_hbm.at[hbm_row], xbuf.at[buf, r], sem_g.at[buf])

    def scatter_copy(hbm_row, buf, r):
        return pltpu.make_async_copy(ybuf.at[buf, r], out_hbm.at[hbm_row], sem_s.at[buf])

    def start_all(make, idx_s, tile, buf):
        for r in range(tm):
            make(idx_s[tile * tm + r], buf, r).start()

    def wait_all(make, buf):
        for r in range(tm):
            make(0, buf, r).wait()

    @pl.when(i == 0)
    def _():
        start_all(gather_copy, tos_s, 0, 0)
        ybuf[...] = jnp.zeros(ybuf.shape, F32)
        n_real = out_hbm.shape[0] - 2 * tm
        fills = [pltpu.make_async_copy(ybuf.at[b], out_hbm.at[pl.ds(n_real + b * tm, tm)], sem_s.at[b])
                 for b in range(2)]
        for cp in fills:
            cp.start()
        for cp in fills:
            cp.wait()

    @pl.when(i < n_used)
    def _():
        wait_all(gather_copy, slot)

        start_all(gather_copy, tos_s, jnp.minimum(i + 1, pl.num_programs(0) - 1), 1 - slot)

        x = jnp.concatenate([xbuf[slot, :, c, :] for c in range(ROW_SUB)], axis=-1).astype(BF16)
        y = _swiglu_rows(x, wg_ref, wu_ref, wd_ref, lead=(0,)) * ws_ref[...]

        @pl.when(i >= 2)
        def _():
            wait_all(scatter_copy, slot)

        for c in range(ROW_SUB):
            ybuf[slot, :, c, :] = y[:, c * LANES:(c + 1) * LANES]
        start_all(scatter_copy, dst_s, i, slot)

        @pl.when(i == n_used - 1)
        def _():
            wait_all(gather_copy, 1 - slot)
            wait_all(scatter_copy, slot)

            @pl.when(n_used >= 2)
            def _():
                wait_all(scatter_copy, 1 - slot)


def _experts(f3, ws, tile_expert, n_used, tos, dst, w_g, w_u, w_d, *, tm, n_out):
    n_slots = ws.shape[0]
    D = w_g.shape[1]
    d_ff = w_g.shape[-1]
    grid_spec = pltpu.PrefetchScalarGridSpec(
        num_scalar_prefetch=4,
        grid=(n_slots // tm,),
        in_specs=[pl.BlockSpec(memory_space=pl.ANY),
                  pl.BlockSpec((tm, 1), lambda i, te, *_: (i, 0)),
                  pl.BlockSpec((1, D, d_ff), lambda i, te, *_: (te[i], 0, 0)),
                  pl.BlockSpec((1, D, d_ff), lambda i, te, *_: (te[i], 0, 0)),
                  pl.BlockSpec((1, d_ff, D), lambda i, te, *_: (te[i], 0, 0))],
        out_specs=pl.BlockSpec(memory_space=pl.ANY),
        scratch_shapes=[pltpu.VMEM((2, tm, ROW_SUB, LANES), F32),
                        pltpu.VMEM((2, tm, ROW_SUB, LANES), F32),
                        pltpu.SemaphoreType.DMA((2,)),
                        pltpu.SemaphoreType.DMA((2,))],
    )
    return pl.pallas_call(
        _expert_kernel,
        grid_spec=grid_spec,
        out_shape=jax.ShapeDtypeStruct((n_out, ROW_SUB, LANES), F32),
        compiler_params=pltpu.CompilerParams(
            dimension_semantics=("arbitrary",), vmem_limit_bytes=VMEM_LIMIT),
        name="experts",
    )(tile_expert, n_used, tos, dst, f3, ws, w_g, w_u, w_d)


def _final_kernel(h_ref, y1_ref, y2_ref, ng_ref, o_ref):
    moe = jnp.concatenate([y1_ref[:, c, :] + y2_ref[:, c, :] for c in range(ROW_SUB)], axis=-1)
    hh = h_ref[...] + moe
    ms = jnp.mean(hh * hh, axis=-1, keepdims=True)
    o_ref[...] = hh * lax.rsqrt(ms + EPS) * ng_ref[...]


def _final(h, y3, ng, *, tb):
    T, D = h.shape
    nblk = T // tb
    return pl.pallas_call(
        _final_kernel,
        grid=(nblk,),
        in_specs=[pl.BlockSpec((tb, D), lambda i: (i, 0)),
                  pl.BlockSpec((tb, ROW_SUB, LANES), lambda i: (i, 0, 0)),
                  pl.BlockSpec((tb, ROW_SUB, LANES), lambda i: (i + nblk, 0, 0)),
                  _const_spec((1, D))],
        out_specs=pl.BlockSpec((tb, D), lambda i: (i, 0)),
        out_shape=jax.ShapeDtypeStruct((T, D), F32),
        compiler_params=pltpu.CompilerParams(
            dimension_semantics=("arbitrary",), vmem_limit_bytes=VMEM_LIMIT),
        name="final",
    )(h, y3, y3, ng.reshape(1, D))


def _moe_and_final_norm(h, ng, w_router, w_g, w_u, w_d, final_g, *, tm, tb):
    T, D = h.shape
    tb = min(tb, T)
    tm = min(tm, tb)
    nblk = T // tb
    E = N_EXPERTS
    assert D == ROW_SUB * LANES and T % tb == 0
    n_tiles = (2 * T) // tm + E
    n_slots = n_tiles * tm
    f3, meta, blk_cnt = _router(h, ng, w_router, tb=tb)

    e12 = meta[:, 0:2].astype(jnp.int32)
    r12 = meta[:, 2:4].astype(jnp.int32)
    tot = jnp.sum(blk_cnt[:, 0, :E], axis=0).astype(jnp.int32)
    padded = ((tot + tm - 1) // tm) * tm
    ends = jnp.cumsum(padded)
    off = ends - padded
    sel = e12[:, :, None] == jnp.arange(E, dtype=jnp.int32)
    slots = jnp.sum(jnp.where(sel, off, 0), axis=-1) + r12
    tile_start = jnp.arange(n_tiles, dtype=jnp.int32) * tm
    tile_expert = jnp.minimum(jnp.sum(ends[None, :] <= tile_start[:, None], axis=1), E - 1).astype(jnp.int32)
    n_used = (ends[-1] // tm).reshape(1).astype(jnp.int32)

    tok = jnp.arange(T, dtype=jnp.int32)
    pos = jnp.arange(n_slots, dtype=jnp.int32)
    scratch_row = 2 * T + ((pos // tm) % 2) * tm + pos % tm
    base = jnp.stack([jnp.zeros((n_slots,), F32), scratch_row.astype(F32), jnp.zeros((n_slots,), F32)], axis=1)
    dest = jnp.stack([tok, tok + T], axis=1).astype(F32)
    payload = jnp.stack([jnp.stack([tok, tok], axis=1).astype(F32), dest, meta[:, 4:6]], axis=2).reshape(2 * T, 3)
    table = base.at[slots.reshape(-1)].set(payload, unique_indices=True)
    tos = table[:, 0].astype(jnp.int32)
    dst = table[:, 1].astype(jnp.int32)
    ws = table[:, 2:3]

    y3 = _experts(f3, ws, tile_expert, n_used, tos, dst, w_g.astype(BF16), w_u.astype(BF16), w_d.astype(BF16),
                  tm=tm, n_out=2 * T + 2 * tm)
    return _final(h, y3, final_g, tb=tb)


def kernel(x, attn_norm_g, w_in, sc_conv_w, cf_conv_w, cf_conv_b, cf_ln_g, cf_ln_b, gla_w_a2, gla_b_a,
           gla_norm_g, w_out, ffn_norm_g, dense_w_gate, dense_w_up, dense_w_down, moe_w_router,
           moe_w_gate, moe_w_up, moe_w_down, final_norm_g):
    B, S, D = x.shape
    depth = w_in.shape[0]
    assert depth == 2, "layer schedule below is dense FFN then routed experts"
    h = x
    for layer in range(depth):
        h = _mixer(h, attn_norm_g[layer], w_in[layer], sc_conv_w[layer], cf_conv_w[layer], cf_conv_b[layer],
                   cf_ln_g[layer], cf_ln_b[layer], gla_w_a2[layer], gla_b_a[layer], gla_norm_g[layer],
                   w_out[layer], ts=256)
        h2 = h.reshape(B * S, D)
        if layer % 2 == 0:
            h = _dense_ffn(h2, ffn_norm_g[layer], dense_w_gate[0], dense_w_up[0], dense_w_down[0],
                           tm=512).reshape(B, S, D)
        else:
            h = _moe_and_final_norm(h2, ffn_norm_g[layer], moe_w_router[0], moe_w_gate[0], moe_w_up[0],
                                    moe_w_down[0], final_norm_g, tm=512, tb=512).reshape(B, S, D)
    return h
```

```python
import functools

import numpy as np
import jax
import jax.numpy as jnp
from jax import lax
from jax.experimental import pallas as pl
from jax.experimental.pallas import tpu as pltpu

F32 = jnp.float32
BF16 = jnp.bfloat16

EPS = 1e-6
SC_K = 3
CF_K = 31
GLA_H = 4
GLA_TAU = 16.0
CHUNK = 64
GATE_STEP_BOUND = 1.0
N_EXPERTS = 8
LANES = 128
ROW_SUB = 8
VMEM_LIMIT = 56 * 1024 * 1024

W = 256
C_SCB, C_SCC, C_SCV, C_CFA, C_CFG, C_Q, C_K = (i * W for i in range(7))
C_V = 7 * W
C_R = C_V + 2 * W
C_ALR = C_R + 2 * W
D_INP = C_ALR + LANES


def _split_hi_lo(x):
    hi = x.astype(BF16)
    lo = (x - hi.astype(F32)).astype(BF16)
    return hi, lo


def _dot(a, b):
    return jnp.dot(a, b, preferred_element_type=F32)


def _dot_nt(a, b):
    return lax.dot_general(a, b, (((1,), (1,)), ((), ())), preferred_element_type=F32)


def _dot_tn(a, b):
    return lax.dot_general(a, b, (((0,), (0,)), ((), ())), preferred_element_type=F32)


def _level_matrices():
    mats = [np.tril(np.ones((CHUNK, CHUNK), np.float32))]
    s = CHUNK // 2
    while s >= 1:
        m = np.zeros((CHUNK, CHUNK), np.float32)
        for i in range(CHUNK):
            mid = (i // (2 * s)) * 2 * s + s
            if i >= mid:
                m[i, mid:i + 1] = 1.0
            else:
                m[i, i + 1:mid] = 1.0
        mats.append(m)
        s //= 2
    return np.concatenate(mats, axis=0)


N_LEVELS = int(np.log2(CHUNK))


def _gla_chunk(c, p_scr, g_scr, y_scr, st_scr, gng, mall, *, bounded):
    dk = W // GLA_H
    dv = 2 * W // GLA_H
    if isinstance(c, int):
        rows = slice(c * CHUNK, (c + 1) * CHUNK)
    else:
        rows = pl.ds(pl.multiple_of(c * CHUNK, CHUNK), CHUNK)
    q = p_scr[rows, C_Q:C_Q + W] * (dk ** -0.5)
    k = p_scr[rows, C_K:C_K + W]
    v = p_scr[rows, C_V:C_V + 2 * W].astype(BF16)
    r = p_scr[rows, C_R:C_R + 2 * W]
    g_hi, g_lo = _split_hi_lo(g_scr[rows, :])

    dk_sh = dk.bit_length() - 1
    dv_sh = dv.bit_length() - 1
    row = lax.broadcasted_iota(jnp.int32, (CHUNK, W), 0)
    lane_head = lax.broadcasted_iota(jnp.int32, (CHUNK, W), 1) >> dk_sh
    srow = lax.broadcasted_iota(jnp.int32, (GLA_H * CHUNK, CHUNK), 0) & (CHUNK - 1)
    scol = lax.broadcasted_iota(jnp.int32, (GLA_H * CHUNK, CHUNK), 1)

    def stack_heads(x):
        xb = x.astype(BF16)
        zero = jnp.zeros_like(xb)
        return jnp.concatenate([jnp.where(lane_head == h, xb, zero) for h in range(GLA_H)], axis=0)

    if bounded:
        cum = mall[0:CHUNK]
        b = _dot(cum, g_hi) + _dot(cum, g_lo)
        ql = q * jnp.exp(b)
        kl = k * jnp.exp(-b)
        scores = jnp.where(srow >= scol, _dot_nt(stack_heads(ql), kl.astype(BF16)), 0.0)
    else:
        ex = _dot(mall, g_hi) + _dot(mall, g_lo)
        b = ex[0:CHUNK]
        scores = jnp.where(srow == scol, _dot_nt(stack_heads(q), k.astype(BF16)), 0.0)
        for lvl in range(1, N_LEVELS + 1):
            s = CHUNK >> lvl
            sh = s.bit_length() - 1
            e = jnp.exp(ex[lvl * CHUNK:(lvl + 1) * CHUNK])
            second = ((row >> sh) & 1) == 1
            ql = jnp.where(second, q * e, 0.0)
            kl = jnp.where(second, 0.0, k * e)
            same = (srow >> (sh + 1)) == (scol >> (sh + 1))
            scores = scores + jnp.where(same, _dot_nt(stack_heads(ql), kl.astype(BF16)), 0.0)
    b_last = b[CHUNK - 1:CHUNK]

    vhead = lax.broadcasted_iota(jnp.int32, (CHUNK, 2 * W), 1) >> dv_sh
    o_stack = _dot(scores.astype(BF16), v)
    q_in = (q * jnp.exp(b)).astype(BF16)
    st = st_scr[...]
    o = _dot_nt(q_in, st.astype(BF16))
    for h in range(GLA_H):
        o = o + jnp.where(vhead == h, o_stack[h * CHUNK:(h + 1) * CHUNK], 0.0)

    k_s = (k * jnp.exp(b_last - b)).astype(BF16)
    upd = _dot_tn(v, k_s)
    rhead = lax.broadcasted_iota(jnp.int32, (2 * W, W), 0) >> dv_sh
    chead = lax.broadcasted_iota(jnp.int32, (2 * W, W), 1) >> dk_sh
    st_scr[...] = st * jnp.exp(b_last) + jnp.where(rhead == chead, upd, 0.0)

    for h in range(GLA_H):
        oh = o[:, h * dv:(h + 1) * dv]
        ms = jnp.mean(oh * oh, axis=-1, keepdims=True)
        rh = r[:, h * dv:(h + 1) * dv]
        yh = oh * lax.rsqrt(ms + EPS) * gng[:, h * dv:(h + 1) * dv] * (rh * jax.nn.sigmoid(rh))
        y_scr[rows, 2 * W + h * dv:2 * W + (h + 1) * dv] = yh.astype(BF16)


def _mixer_kernel(x_ref, ng_ref, win_ref, scw_ref, cfw_ref, cfb_ref, lng_ref, lnb_ref,
                  wa2_ref, ba_ref, gng_ref, mall_ref, wout_ref, o_ref,
                  p_scr, sc_buf, cf_buf, st_scr, y_scr, g_scr, *, ts):
    sc_halo = 8
    cf_halo = 32

    @pl.when(pl.program_id(1) == 0)
    def _():
        sc_buf[0:sc_halo, :] = jnp.zeros((sc_halo, W), F32)
        cf_buf[0:cf_halo, :] = jnp.zeros((cf_halo, W), F32)
        st_scr[...] = jnp.zeros(st_scr.shape, F32)

    x = x_ref[0]
    ms = jnp.mean(x * x, axis=-1, keepdims=True)
    a = (x * lax.rsqrt(ms + EPS) * ng_ref[...]).astype(BF16)
    p_scr[...] = _dot(a, win_ref[...])

    sc_buf[sc_halo:sc_halo + ts, :] = p_scr[:, C_SCC:C_SCC + W] * p_scr[:, C_SCV:C_SCV + W]
    cf_buf[cf_halo:cf_halo + ts, :] = p_scr[:, C_CFA:C_CFA + W] * jax.nn.sigmoid(p_scr[:, C_CFG:C_CFG + W])
    scw = scw_ref[...]
    cfw = cfw_ref[...]
    for c in range(ts // CHUNK):
        r0 = c * CHUNK
        conv = jnp.zeros((CHUNK, W), F32)
        for j in range(SC_K):
            o0 = r0 + sc_halo - (SC_K - 1) + j
            conv = conv + sc_buf[o0:o0 + CHUNK, :] * scw[j:j + 1]
        y_scr[r0:r0 + CHUNK, 0:W] = (p_scr[r0:r0 + CHUNK, C_SCB:C_SCB + W] * conv).astype(BF16)

        u = jnp.zeros((CHUNK, W), F32) + cfb_ref[...]
        tap0 = cf_halo - (CF_K - 1)
        for rr in range(ROW_SUB):
            n_rows = CHUNK if rr == 0 else CHUNK + ROW_SUB
            part = None
            for j in range(CF_K):
                if (tap0 + j) % ROW_SUB != rr:
                    continue
                a0 = r0 + (tap0 + j) // ROW_SUB * ROW_SUB
                term = cf_buf[a0:a0 + n_rows, :] * cfw[j:j + 1]
                part = term if part is None else part + term
            u = u + part[rr:rr + CHUNK]
        mu = jnp.mean(u, axis=-1, keepdims=True)
        d = u - mu
        var = jnp.mean(d * d, axis=-1, keepdims=True)
        yn = d * lax.rsqrt(var + EPS) * lng_ref[...] + lnb_ref[...]
        y_scr[r0:r0 + CHUNK, W:2 * W] = (yn * jax.nn.sigmoid(yn)).astype(BF16)
    sc_buf[0:sc_halo, :] = sc_buf[ts:ts + sc_halo, :]
    cf_buf[0:cf_halo, :] = cf_buf[ts:ts + cf_halo, :]

    wa2_hi, wa2_lo = _split_hi_lo(wa2_ref[...])
    alr_hi, alr_lo = _split_hi_lo(p_scr[:, C_ALR:C_ALR + LANES])
    z = _dot(alr_hi, wa2_hi) + _dot(alr_lo, wa2_hi) + _dot(alr_hi, wa2_lo) + ba_ref[...]
    g_scr[...] = (jnp.minimum(z, 0.0) - jnp.log1p(jnp.exp(-jnp.abs(z)))) * (1.0 / GLA_TAU)
    gng = gng_ref[...]
    mall = mall_ref[...]
    bounded = jnp.min(g_scr[...]) >= -GATE_STEP_BOUND

    @pl.when(bounded)
    def _():
        for c in range(ts // CHUNK):
            _gla_chunk(c, p_scr, g_scr, y_scr, st_scr, gng, mall, bounded=True)

    @pl.when(jnp.logical_not(bounded))
    def _():
        def body(c, carry):
            _gla_chunk(c, p_scr, g_scr, y_scr, st_scr, gng, mall, bounded=False)
            return carry
        lax.fori_loop(0, ts // CHUNK, body, 0)

    o_ref[0] = x + _dot(y_scr[...], wout_ref[...])


def _const_spec(shape):
    nd = len(shape)
    return pl.BlockSpec(shape, lambda *_: (0,) * nd)


def _mixer(x, ng, w_in, sc_w, cf_w, cf_b, ln_g, ln_b, w_a2, b_a, gn_g, w_out, *, ts):
    B, S, D = x.shape
    ts = min(ts, S)
    split = np.cumsum([0, W, W, W, W, W, W, W, 2 * W, 16, 2 * W])
    cols = [w_in[:, split[i]:split[i + 1]] for i in range(10)]
    w_alr = jnp.pad(cols[8], ((0, 0), (0, LANES - 16)))
    w_in_r = jnp.concatenate(cols[:8] + [cols[9], w_alr], axis=1).astype(BF16)
    scw = jnp.pad(sc_w, ((0, 8 - SC_K), (0, 0)))
    cfw = jnp.pad(cf_w, ((0, 32 - CF_K), (0, 0)))
    wa2 = jnp.pad(w_a2, ((0, LANES - 16), (0, 0)))
    mall = jnp.asarray(_level_matrices(), BF16)
    consts = [ng.reshape(1, D), w_in_r, scw, cfw, cf_b.reshape(1, W), ln_g.reshape(1, W),
              ln_b.reshape(1, W), wa2, b_a.reshape(1, W), gn_g.reshape(1, 2 * W), mall,
              w_out.astype(BF16)]
    return pl.pallas_call(
        functools.partial(_mixer_kernel, ts=ts),
        grid=(B, S // ts),
        in_specs=[pl.BlockSpec((1, ts, D), lambda b, t: (b, t, 0))] + [_const_spec(c.shape) for c in consts],
        out_specs=pl.BlockSpec((1, ts, D), lambda b, t: (b, t, 0)),
        out_shape=jax.ShapeDtypeStruct((B, S, D), F32),
        scratch_shapes=[
            pltpu.VMEM((ts, D_INP), F32),
            pltpu.VMEM((ts + 8, W), F32),
            pltpu.VMEM((ts + 32, W), F32),
            pltpu.VMEM((2 * W, W), F32),
            pltpu.VMEM((ts, D), BF16),
            pltpu.VMEM((ts, W), F32),
        ],
        compiler_params=pltpu.CompilerParams(
            dimension_semantics=("arbitrary", "arbitrary"), vmem_limit_bytes=VMEM_LIMIT),
        name="mixer",
    )(x, *consts)


def _ff_chunks(d_ff, step=1024):
    return [(s, min(s + step, d_ff)) for s in range(0, d_ff, step)]


def _swiglu_rows(f, wg_ref, wu_ref, wd_ref, lead=()):
    d_ff = wg_ref.shape[-1]
    acc = None
    for s, e in _ff_chunks(d_ff):
        gate = _dot(f, wg_ref[lead + (slice(None), slice(s, e))])
        up = _dot(f, wu_ref[lead + (slice(None), slice(s, e))])
        act = (gate * jax.nn.sigmoid(gate) * up).astype(BF16)
        part = _dot(act, wd_ref[lead + (slice(s, e), slice(None))])
        acc = part if acc is None else acc + part
    return acc


def _ffn_kernel(h_ref, ng_ref, wg_ref, wu_ref, wd_ref, o_ref):
    h = h_ref[...]
    ms = jnp.mean(h * h, axis=-1, keepdims=True)
    f = (h * lax.rsqrt(ms + EPS) * ng_ref[...]).astype(BF16)
    o_ref[...] = h + _swiglu_rows(f, wg_ref, wu_ref, wd_ref)


def _dense_ffn(h, ng, w_g, w_u, w_d, *, tm):
    T, D = h.shape
    tm = min(tm, T)
    consts = [ng.reshape(1, D), w_g.astype(BF16), w_u.astype(BF16), w_d.astype(BF16)]
    return pl.pallas_call(
        _ffn_kernel,
        grid=(T // tm,),
        in_specs=[pl.BlockSpec((tm, D), lambda i: (i, 0))] + [_const_spec(c.shape) for c in consts],
        out_specs=pl.BlockSpec((tm, D), lambda i: (i, 0)),
        out_shape=jax.ShapeDtypeStruct((T, D), F32),
        compiler_params=pltpu.CompilerParams(
            dimension_semantics=("arbitrary",), vmem_limit_bytes=VMEM_LIMIT),
        name="dense_ffn",
    )(h, *consts)


def _router_kernel(h_ref, ng_ref, wr_ref, f_ref, meta_ref, route_ref, cnt_ref, carry):
    tb = h_ref.shape[0]

    @pl.when(pl.program_id(0) == 0)
    def _():
        carry[...] = jnp.zeros(carry.shape, F32)

    h = h_ref[...]
    ms = jnp.mean(h * h, axis=-1, keepdims=True)
    f = h * lax.rsqrt(ms + EPS) * ng_ref[...]
    for c in range(ROW_SUB):
        f_ref[:, c, :] = f[:, c * LANES:(c + 1) * LANES]
    f_hi, f_lo = _split_hi_lo(f)
    w_hi, w_lo = _split_hi_lo(wr_ref[...])
    logits = _dot(f_hi, w_hi) + _dot(f_lo, w_hi) + _dot(f_hi, w_lo)
    lane = lax.broadcasted_iota(jnp.int32, (tb, LANES), 1).astype(F32)
    neg = jnp.float32(-jnp.inf)
    logits = jnp.where(lane < N_EXPERTS, logits, neg)
    m1 = jnp.max(logits, axis=-1, keepdims=True)
    i1 = jnp.min(jnp.where(logits == m1, lane, float(LANES)), axis=-1, keepdims=True)
    rest = jnp.where(lane == i1, neg, logits)
    m2 = jnp.max(rest, axis=-1, keepdims=True)
    i2 = jnp.min(jnp.where(rest == m2, lane, float(LANES)), axis=-1, keepdims=True)
    e21 = jnp.exp(m2 - m1)
    g1 = 1.0 / (1.0 + e21)
    g2 = e21 * g1

    sel1 = lane == i1
    sel2 = lane == i2
    sel = jnp.where(sel1 | sel2, 1.0, 0.0)
    tri = lax.broadcasted_iota(jnp.int32, (tb, tb), 0) > lax.broadcasted_iota(jnp.int32, (tb, tb), 1)
    before = _dot(jnp.where(tri, 1.0, 0.0).astype(BF16), sel.astype(BF16)) + carry[...]
    r1 = jnp.sum(jnp.where(sel1, before, 0.0), axis=-1, keepdims=True)
    r2 = jnp.sum(jnp.where(sel2, before, 0.0), axis=-1, keepdims=True)
    blk_cnt = jnp.sum(sel, axis=0, keepdims=True)
    carry[...] = carry[...] + blk_cnt
    cnt_ref[0] = jnp.broadcast_to(blk_cnt, (8, LANES))

    vals = [i1, i2, r1, r2, g1, g2]
    meta = jnp.zeros((tb, LANES), F32)
    for j, val in enumerate(vals):
        meta = jnp.where(lane == j, val, meta)
    meta_ref[...] = meta
    route_ref[0] = meta.T[0:8, :]


def _router(h, ng, w_router, *, tb):
    T, D = h.shape
    nblk = T // tb
    wr = jnp.pad(w_router, ((0, 0), (0, LANES - N_EXPERTS)))
    return pl.pallas_call(
        _router_kernel,
        grid=(nblk,),
        in_specs=[pl.BlockSpec((tb, D), lambda i: (i, 0)), _const_spec((1, D)), _const_spec(wr.shape)],
        out_specs=[pl.BlockSpec((tb, ROW_SUB, LANES), lambda i: (i, 0, 0)),
                   pl.BlockSpec((tb, LANES), lambda i: (i, 0)),
                   pl.BlockSpec((1, 8, tb), lambda i: (i, 0, 0)),
                   pl.BlockSpec((1, 8, LANES), lambda i: (i, 0, 0))],
        out_shape=[jax.ShapeDtypeStruct((T, ROW_SUB, LANES), F32),
                   jax.ShapeDtypeStruct((T, LANES), F32),
                   jax.ShapeDtypeStruct((nblk, 8, tb), F32),
                   jax.ShapeDtypeStruct((nblk, 8, LANES), F32)],
        scratch_shapes=[pltpu.VMEM((1, LANES), F32)],
        compiler_params=pltpu.CompilerParams(
            dimension_semantics=("arbitrary",), vmem_limit_bytes=VMEM_LIMIT),
        name="router",
    )(h, ng.reshape(1, D), wr)


def _expert_kernel(te_s, nused_s, slots_s, tot_s, ends_s, f_hbm, wg_ref, wu_ref, wd_ref, out_hbm,
                   xbuf, ybuf, enc_s, sem_g, sem_s, *, n_tok, tb):
    i = pl.program_id(0)
    n_used = nused_s[0]
    tm = xbuf.shape[1]
    slot = i % 2
    unroll = 16

    def gather_copy(hbm_row, buf, r):
        return pltpu.make_async_copy(f_hbm.at[hbm_row], xbuf.at[buf, r], sem_g.at[buf])

    def scatter_copy(hbm_row, buf, r):
        return pltpu.make_async_copy(ybuf.at[buf, r], out_hbm.at[hbm_row], sem_s.at[buf])

    def start_gather(tile, buf):
        for r in range(tm):
            gather_copy(enc_s[tile * tm + r] & (n_tok - 1), buf, r).start()

    def start_scatter(tile, buf):
        for r in range(tm):
            scatter_copy(enc_s[tile * tm + r], buf, r).start()

    def wait_all(make, buf):
        for r in range(tm):
            make(0, buf, r).wait()

    @pl.when(i == 0)
    def _():
        for e in range(N_EXPERTS):
            end = ends_s[e]
            first_pad = end - (tot_s[e] + tm - 1) // tm * tm + tot_s[e]
            row0 = 2 * n_tok + ((end // tm - 1) % 2) * tm - (end - tm)

            def pad_body(j, carry, row0=row0):
                enc_s[j] = row0 + j
                return carry

            lax.fori_loop(first_pad, end, pad_body, 0)

        def fill_block(blk, carry):
            def fill_body(it, carry):
                src = blk * 2 * tb + it * unroll
                tok = blk * tb + it * unroll
                for u in range(unroll):
                    enc_s[slots_s[src + u]] = tok + u
                    enc_s[slots_s[src + tb + u]] = tok + n_tok + u
                return carry

            return lax.fori_loop(0, tb // unroll, fill_body, carry)

        lax.fori_loop(0, n_tok // tb, fill_block, 0)

        start_gather(0, 0)
        ybuf[...] = jnp.zeros(ybuf.shape, F32)
        fills = [pltpu.make_async_copy(ybuf.at[b], out_hbm.at[pl.ds(2 * n_tok + b * tm, tm)], sem_s.at[b])
                 for b in range(2)]
        for cp in fills:
            cp.start()
        for cp in fills:
            cp.wait()

    @pl.when(i < n_used)
    def _():
        wait_all(gather_copy, slot)

        start_gather(jnp.minimum(i + 1, n_used - 1), 1 - slot)

        x = jnp.concatenate([xbuf[slot, :, c, :] for c in range(ROW_SUB)], axis=-1).astype(BF16)
        y = _swiglu_rows(x, wg_ref, wu_ref, wd_ref, lead=(0,))

        @pl.when(i >= 2)
        def _():
            wait_all(scatter_copy, slot)

        for c in range(ROW_SUB):
            ybuf[slot, :, c, :] = y[:, c * LANES:(c + 1) * LANES]
        start_scatter(i, slot)

        @pl.when(i == n_used - 1)
        def _():
            wait_all(gather_copy, 1 - slot)
            wait_all(scatter_copy, slot)

            @pl.when(n_used >= 2)
            def _():
                wait_all(scatter_copy, 1 - slot)


def _experts(f3, tile_expert, n_used, slots, tot, ends, w_g, w_u, w_d, *, tm, tb):
    n_tok = f3.shape[0]
    n_tiles = tile_expert.shape[0]
    D = w_g.shape[1]
    d_ff = w_g.shape[-1]
    grid_spec = pltpu.PrefetchScalarGridSpec(
        num_scalar_prefetch=5,
        grid=(n_tiles,),
        in_specs=[pl.BlockSpec(memory_space=pl.ANY),
                  pl.BlockSpec((1, D, d_ff), lambda i, te, *_: (te[i], 0, 0)),
                  pl.BlockSpec((1, D, d_ff), lambda i, te, *_: (te[i], 0, 0)),
                  pl.BlockSpec((1, d_ff, D), lambda i, te, *_: (te[i], 0, 0))],
        out_specs=pl.BlockSpec(memory_space=pl.ANY),
        scratch_shapes=[pltpu.VMEM((2, tm, ROW_SUB, LANES), F32),
                        pltpu.VMEM((2, tm, ROW_SUB, LANES), F32),
                        pltpu.SMEM((n_tiles * tm,), jnp.int32),
                        pltpu.SemaphoreType.DMA((2,)),
                        pltpu.SemaphoreType.DMA((2,))],
    )
    return pl.pallas_call(
        functools.partial(_expert_kernel, n_tok=n_tok, tb=tb),
        grid_spec=grid_spec,
        out_shape=jax.ShapeDtypeStruct((2 * n_tok + 2 * tm, ROW_SUB, LANES), F32),
        compiler_params=pltpu.CompilerParams(
            dimension_semantics=("arbitrary",), vmem_limit_bytes=VMEM_LIMIT),
        name="experts",
    )(tile_expert, n_used, slots, tot, ends, f3, w_g, w_u, w_d)


def _final_kernel(h_ref, meta_ref, y1_ref, y2_ref, ng_ref, o_ref):
    g1 = meta_ref[:, 4:5]
    g2 = meta_ref[:, 5:6]
    moe = jnp.concatenate([g1 * y1_ref[:, c, :] + g2 * y2_ref[:, c, :] for c in range(ROW_SUB)], axis=-1)
    hh = h_ref[...] + moe
    ms = jnp.mean(hh * hh, axis=-1, keepdims=True)
    o_ref[...] = hh * lax.rsqrt(ms + EPS) * ng_ref[...]


def _final(h, meta, y3, ng, *, tb):
    T, D = h.shape
    nblk = T // tb
    return pl.pallas_call(
        _final_kernel,
        grid=(nblk,),
        in_specs=[pl.BlockSpec((tb, D), lambda i: (i, 0)),
                  pl.BlockSpec((tb, LANES), lambda i: (i, 0)),
                  pl.BlockSpec((tb, ROW_SUB, LANES), lambda i: (i, 0, 0)),
                  pl.BlockSpec((tb, ROW_SUB, LANES), lambda i: (i + nblk, 0, 0)),
                  _const_spec((1, D))],
        out_specs=pl.BlockSpec((tb, D), lambda i: (i, 0)),
        out_shape=jax.ShapeDtypeStruct((T, D), F32),
        compiler_params=pltpu.CompilerParams(
            dimension_semantics=("arbitrary",), vmem_limit_bytes=VMEM_LIMIT),
        name="final",
    )(h, meta, y3, y3, ng.reshape(1, D))


def _moe_and_final_norm(h, ng, w_router, w_g, w_u, w_d, final_g, *, tm, tb):
    T, D = h.shape
    tb = min(tb, T)
    tm = min(tm, tb)
    E = N_EXPERTS
    assert D == ROW_SUB * LANES and T % tb == 0 and T & (T - 1) == 0
    n_tiles = (2 * T) // tm + E
    f3, meta, route, blk_cnt = _router(h, ng, w_router, tb=tb)

    tot = jnp.sum(blk_cnt[:, 0, :E], axis=0).astype(jnp.int32)
    padded = (tot + tm - 1) // tm * tm
    ends = jnp.cumsum(padded)
    off = ends - padded
    route = route.astype(jnp.int32)
    e12 = route[:, 0:2, :]
    slots = route[:, 2:4, :]
    for e in range(E):
        slots = slots + jnp.where(e12 == e, off[e], 0)
    tile_start = jnp.arange(n_tiles, dtype=jnp.int32) * tm
    tile_expert = jnp.minimum(jnp.sum(ends[None, :] <= tile_start[:, None], axis=1), E - 1).astype(jnp.int32)
    n_used = (ends[-1] // tm).reshape(1)

    y3 = _experts(f3, tile_expert, n_used, slots.reshape(-1), tot, ends,
                  w_g.astype(BF16), w_u.astype(BF16), w_d.astype(BF16), tm=tm, tb=tb)
    return _final(h, meta, y3, final_g, tb=tb)


def kernel(x, attn_norm_g, w_in, sc_conv_w, cf_conv_w, cf_conv_b, cf_ln_g, cf_ln_b, gla_w_a2, gla_b_a,
           gla_norm_g, w_out, ffn_norm_g, dense_w_gate, dense_w_up, dense_w_down, moe_w_router,
           moe_w_gate, moe_w_up, moe_w_down, final_norm_g):
    B, S, D = x.shape
    depth = w_in.shape[0]
    assert depth == 2, "layer schedule below is dense FFN then routed experts"
    h = x
    for layer in range(depth):
        h = _mixer(h, attn_norm_g[layer], w_in[layer], sc_conv_w[layer], cf_conv_w[layer], cf_conv_b[layer],
                   cf_ln_g[layer], cf_ln_b[layer], gla_w_a2[layer], gla_b_a[layer], gla_norm_g[layer],
                   w_out[layer], ts=256)
        h2 = h.reshape(B * S, D)
        if layer % 2 == 0:
            h = _dense_ffn(h2, ffn_norm_g[layer], dense_w_gate[0], dense_w_up[0], dense_w_down[0],
                           tm=512).reshape(B, S, D)
        else:
            h = _moe_and_final_norm(h2, ffn_norm_g[layer], moe_w_router[0], moe_w_gate[0], moe_w_up[0],
                                    moe_w_down[0], final_norm_g, tm=512, tb=512).reshape(B, S, D)
    return h
```

```python
import functools

import numpy as np
import jax
import jax.numpy as jnp
from jax import lax
from jax.experimental import pallas as pl
from jax.experimental.pallas import tpu as pltpu

F32 = jnp.float32
BF16 = jnp.bfloat16

EPS = 1e-6
SC_K = 3
CF_K = 31
GLA_H = 4
GLA_TAU = 16.0
CHUNK = 64
GATE_STEP_BOUND = 1.0
N_EXPERTS = 8
LANES = 128
ROW_SUB = 8
VMEM_LIMIT = 56 * 1024 * 1024

W = 256
C_SCB, C_SCC, C_SCV, C_CFA, C_CFG, C_Q, C_K = (i * W for i in range(7))
C_V = 7 * W
C_R = C_V + 2 * W
C_ALR = C_R + 2 * W
D_INP = C_ALR + LANES


def _split_hi_lo(x):
    hi = x.astype(BF16)
    lo = (x - hi.astype(F32)).astype(BF16)
    return hi, lo


def _dot(a, b):
    return jnp.dot(a, b, preferred_element_type=F32)


def _dot_nt(a, b):
    return lax.dot_general(a, b, (((1,), (1,)), ((), ())), preferred_element_type=F32)


def _dot_tn(a, b):
    return lax.dot_general(a, b, (((0,), (0,)), ((), ())), preferred_element_type=F32)


def _level_matrices():
    mats = [np.tril(np.ones((CHUNK, CHUNK), np.float32))]
    s = CHUNK // 2
    while s >= 1:
        m = np.zeros((CHUNK, CHUNK), np.float32)
        for i in range(CHUNK):
            mid = (i // (2 * s)) * 2 * s + s
            if i >= mid:
                m[i, mid:i + 1] = 1.0
            else:
                m[i, i + 1:mid] = 1.0
        mats.append(m)
        s //= 2
    return np.concatenate(mats, axis=0)


N_LEVELS = int(np.log2(CHUNK))


def _gla_chunk(c, p_scr, g_scr, y_scr, st_scr, gng, mall, *, bounded):
    dk = W // GLA_H
    dv = 2 * W // GLA_H
    if isinstance(c, int):
        rows = slice(c * CHUNK, (c + 1) * CHUNK)
    else:
        rows = pl.ds(pl.multiple_of(c * CHUNK, CHUNK), CHUNK)
    q = p_scr[rows, C_Q:C_Q + W] * (dk ** -0.5)
    k = p_scr[rows, C_K:C_K + W]
    v = p_scr[rows, C_V:C_V + 2 * W].astype(BF16)
    r = p_scr[rows, C_R:C_R + 2 * W]
    g_hi, g_lo = _split_hi_lo(g_scr[rows, :])

    dk_sh = dk.bit_length() - 1
    dv_sh = dv.bit_length() - 1
    row = lax.broadcasted_iota(jnp.int32, (CHUNK, W), 0)
    lane_head = lax.broadcasted_iota(jnp.int32, (CHUNK, W), 1) >> dk_sh
    srow = lax.broadcasted_iota(jnp.int32, (GLA_H * CHUNK, CHUNK), 0) & (CHUNK - 1)
    scol = lax.broadcasted_iota(jnp.int32, (GLA_H * CHUNK, CHUNK), 1)

    def stack_heads(x):
        xb = x.astype(BF16)
        zero = jnp.zeros_like(xb)
        return jnp.concatenate([jnp.where(lane_head == h, xb, zero) for h in range(GLA_H)], axis=0)

    if bounded:
        cum = mall[0:CHUNK]
        b = _dot(cum, g_hi) + _dot(cum, g_lo)
        ql = q * jnp.exp(b)
        kl = k * jnp.exp(-b)
        scores = jnp.where(srow >= scol, _dot_nt(stack_heads(ql), kl.astype(BF16)), 0.0)
    else:
        ex = _dot(mall, g_hi) + _dot(mall, g_lo)
        b = ex[0:CHUNK]
        scores = jnp.where(srow == scol, _dot_nt(stack_heads(q), k.astype(BF16)), 0.0)
        for lvl in range(1, N_LEVELS + 1):
            s = CHUNK >> lvl
            sh = s.bit_length() - 1
            e = jnp.exp(ex[lvl * CHUNK:(lvl + 1) * CHUNK])
            second = ((row >> sh) & 1) == 1
            ql = jnp.where(second, q * e, 0.0)
            kl = jnp.where(second, 0.0, k * e)
            same = (srow >> (sh + 1)) == (scol >> (sh + 1))
            scores = scores + jnp.where(same, _dot_nt(stack_heads(ql), kl.astype(BF16)), 0.0)
    b_last = b[CHUNK - 1:CHUNK]

    vhead = lax.broadcasted_iota(jnp.int32, (CHUNK, 2 * W), 1) >> dv_sh
    o_stack = _dot(scores.astype(BF16), v)
    q_in = (q * jnp.exp(b)).astype(BF16)
    st = st_scr[...]
    o = _dot_nt(q_in, st.astype(BF16))
    for h in range(GLA_H):
        o = o + jnp.where(vhead == h, o_stack[h * CHUNK:(h + 1) * CHUNK], 0.0)

    k_s = (k * jnp.exp(b_last - b)).astype(BF16)
    upd = _dot_tn(v, k_s)
    rhead = lax.broadcasted_iota(jnp.int32, (2 * W, W), 0) >> dv_sh
    chead = lax.broadcasted_iota(jnp.int32, (2 * W, W), 1) >> dk_sh
    st_scr[...] = st * jnp.exp(b_last) + jnp.where(rhead == chead, upd, 0.0)

    for h in range(GLA_H):
        oh = o[:, h * dv:(h + 1) * dv]
        ms = jnp.mean(oh * oh, axis=-1, keepdims=True)
        rh = r[:, h * dv:(h + 1) * dv]
        yh = oh * lax.rsqrt(ms + EPS) * gng[:, h * dv:(h + 1) * dv] * (rh * jax.nn.sigmoid(rh))
        y_scr[rows, 2 * W + h * dv:2 * W + (h + 1) * dv] = yh.astype(BF16)


def _mixer_kernel(*refs, ts, n_cast):
    (x_ref, ng_ref, win_ref, scw_ref, cfw_ref, cfb_ref, lng_ref, lnb_ref,
     wa2_ref, ba_ref, gng_ref, mall_ref, wout_ref) = refs[:13]
    cast_in = refs[13:13 + n_cast]
    o_ref = refs[13 + n_cast]
    cast_out = refs[14 + n_cast:14 + 2 * n_cast]
    p_scr, sc_buf, cf_buf, st_scr, y_scr, g_scr = refs[14 + 2 * n_cast:]
    sc_halo = 8
    cf_halo = 32

    for src, dst in zip(cast_in, cast_out):
        dst[...] = src[...].astype(BF16)

    @pl.when(pl.program_id(1) == 0)
    def _():
        sc_buf[0:sc_halo, :] = jnp.zeros((sc_halo, W), F32)
        cf_buf[0:cf_halo, :] = jnp.zeros((cf_halo, W), F32)
        st_scr[...] = jnp.zeros(st_scr.shape, F32)

    x = x_ref[0]
    ms = jnp.mean(x * x, axis=-1, keepdims=True)
    a = (x * lax.rsqrt(ms + EPS) * ng_ref[...]).astype(BF16)
    p_scr[...] = _dot(a, win_ref[...])

    sc_buf[sc_halo:sc_halo + ts, :] = p_scr[:, C_SCC:C_SCC + W] * p_scr[:, C_SCV:C_SCV + W]
    cf_buf[cf_halo:cf_halo + ts, :] = p_scr[:, C_CFA:C_CFA + W] * jax.nn.sigmoid(p_scr[:, C_CFG:C_CFG + W])
    scw = scw_ref[...]
    cfw = cfw_ref[...]
    for c in range(ts // CHUNK):
        r0 = c * CHUNK
        conv = jnp.zeros((CHUNK, W), F32)
        for j in range(SC_K):
            o0 = r0 + sc_halo - (SC_K - 1) + j
            conv = conv + sc_buf[o0:o0 + CHUNK, :] * scw[j:j + 1]
        y_scr[r0:r0 + CHUNK, 0:W] = (p_scr[r0:r0 + CHUNK, C_SCB:C_SCB + W] * conv).astype(BF16)

        u = jnp.zeros((CHUNK, W), F32) + cfb_ref[...]
        tap0 = cf_halo - (CF_K - 1)
        for rr in range(ROW_SUB):
            n_rows = CHUNK if rr == 0 else CHUNK + ROW_SUB
            part = None
            for j in range(CF_K):
                if (tap0 + j) % ROW_SUB != rr:
                    continue
                a0 = r0 + (tap0 + j) // ROW_SUB * ROW_SUB
                term = cf_buf[a0:a0 + n_rows, :] * cfw[j:j + 1]
                part = term if part is None else part + term
            u = u + part[rr:rr + CHUNK]
        mu = jnp.mean(u, axis=-1, keepdims=True)
        d = u - mu
        var = jnp.mean(d * d, axis=-1, keepdims=True)
        yn = d * lax.rsqrt(var + EPS) * lng_ref[...] + lnb_ref[...]
        y_scr[r0:r0 + CHUNK, W:2 * W] = (yn * jax.nn.sigmoid(yn)).astype(BF16)
    sc_buf[0:sc_halo, :] = sc_buf[ts:ts + sc_halo, :]
    cf_buf[0:cf_halo, :] = cf_buf[ts:ts + cf_halo, :]

    wa2_hi, wa2_lo = _split_hi_lo(wa2_ref[...])
    alr_hi, alr_lo = _split_hi_lo(p_scr[:, C_ALR:C_ALR + LANES])
    z = _dot(alr_hi, wa2_hi) + _dot(alr_lo, wa2_hi) + _dot(alr_hi, wa2_lo) + ba_ref[...]
    g_scr[...] = (jnp.minimum(z, 0.0) - jnp.log1p(jnp.exp(-jnp.abs(z)))) * (1.0 / GLA_TAU)
    gng = gng_ref[...]
    mall = mall_ref[...]
    bounded = jnp.min(g_scr[...]) >= -GATE_STEP_BOUND

    @pl.when(bounded)
    def _():
        for c in range(ts // CHUNK):
            _gla_chunk(c, p_scr, g_scr, y_scr, st_scr, gng, mall, bounded=True)

    @pl.when(jnp.logical_not(bounded))
    def _():
        def body(c, carry):
            _gla_chunk(c, p_scr, g_scr, y_scr, st_scr, gng, mall, bounded=False)
            return carry
        lax.fori_loop(0, ts // CHUNK, body, 0)

    o_ref[0] = x + _dot(y_scr[...], wout_ref[...])


def _const_spec(shape):
    nd = len(shape)
    return pl.BlockSpec(shape, lambda *_: (0,) * nd)


def _mixer(x, ng, w_in, sc_w, cf_w, cf_b, ln_g, ln_b, w_a2, b_a, gn_g, w_out, *, ts, cast_jobs=()):
    B, S, D = x.shape
    ts = min(ts, S)
    n_steps = B * (S // ts)
    nt = S // ts
    cast_views = []
    for w in cast_jobs:
        rows = int(np.prod(w.shape[:-1]))
        assert rows % (n_steps * 16) == 0, "cast slice must be whole bf16 row tiles"
        cast_views.append(w.reshape(n_steps, rows // n_steps, w.shape[-1]))
    split = np.cumsum([0, W, W, W, W, W, W, W, 2 * W, 16, 2 * W])
    cols = [w_in[:, split[i]:split[i + 1]] for i in range(10)]
    w_alr = jnp.pad(cols[8], ((0, 0), (0, LANES - 16)))
    w_in_r = jnp.concatenate(cols[:8] + [cols[9], w_alr], axis=1).astype(BF16)
    scw = jnp.pad(sc_w, ((0, 8 - SC_K), (0, 0)))
    cfw = jnp.pad(cf_w, ((0, 32 - CF_K), (0, 0)))
    wa2 = jnp.pad(w_a2, ((0, LANES - 16), (0, 0)))
    mall = jnp.asarray(_level_matrices(), BF16)
    consts = [ng.reshape(1, D), w_in_r, scw, cfw, cf_b.reshape(1, W), ln_g.reshape(1, W),
              ln_b.reshape(1, W), wa2, b_a.reshape(1, W), gn_g.reshape(1, 2 * W), mall,
              w_out.astype(BF16)]
    cast_specs = [pl.BlockSpec((1,) + v.shape[1:], lambda b, t: (b * nt + t, 0, 0)) for v in cast_views]
    outs = pl.pallas_call(
        functools.partial(_mixer_kernel, ts=ts, n_cast=len(cast_views)),
        grid=(B, nt),
        in_specs=([pl.BlockSpec((1, ts, D), lambda b, t: (b, t, 0))] + [_const_spec(c.shape) for c in consts]
                  + cast_specs),
        out_specs=[pl.BlockSpec((1, ts, D), lambda b, t: (b, t, 0))] + cast_specs,
        out_shape=[jax.ShapeDtypeStruct((B, S, D), F32)]
        + [jax.ShapeDtypeStruct(v.shape, BF16) for v in cast_views],
        scratch_shapes=[
            pltpu.VMEM((ts, D_INP), F32),
            pltpu.VMEM((ts + 8, W), F32),
            pltpu.VMEM((ts + 32, W), F32),
            pltpu.VMEM((2 * W, W), F32),
            pltpu.VMEM((ts, D), BF16),
            pltpu.VMEM((ts, W), F32),
        ],
        compiler_params=pltpu.CompilerParams(
            dimension_semantics=("arbitrary", "arbitrary"), vmem_limit_bytes=VMEM_LIMIT),
        name="mixer",
    )(x, *consts, *cast_views)
    return outs[0], [o.reshape(w.shape) for o, w in zip(outs[1:], cast_jobs)]


def _ff_chunks(d_ff, step=1024):
    return [(s, min(s + step, d_ff)) for s in range(0, d_ff, step)]


def _swiglu_rows(f, wg_ref, wu_ref, wd_ref, lead=()):
    d_ff = wg_ref.shape[-1]
    acc = None
    for s, e in _ff_chunks(d_ff):
        gate = _dot(f, wg_ref[lead + (slice(None), slice(s, e))])
        up = _dot(f, wu_ref[lead + (slice(None), slice(s, e))])
        act = (gate * jax.nn.sigmoid(gate) * up).astype(BF16)
        part = _dot(act, wd_ref[lead + (slice(s, e), slice(None))])
        acc = part if acc is None else acc + part
    return acc


def _ffn_kernel(h_ref, ng_ref, wg_ref, wu_ref, wd_ref, o_ref):
    h = h_ref[...]
    ms = jnp.mean(h * h, axis=-1, keepdims=True)
    f = (h * lax.rsqrt(ms + EPS) * ng_ref[...]).astype(BF16)
    o_ref[...] = h + _swiglu_rows(f, wg_ref, wu_ref, wd_ref)


def _dense_ffn(h, ng, w_g, w_u, w_d, *, tm):
    T, D = h.shape
    tm = min(tm, T)
    consts = [ng.reshape(1, D), w_g.astype(BF16), w_u.astype(BF16), w_d.astype(BF16)]
    return pl.pallas_call(
        _ffn_kernel,
        grid=(T // tm,),
        in_specs=[pl.BlockSpec((tm, D), lambda i: (i, 0))] + [_const_spec(c.shape) for c in consts],
        out_specs=pl.BlockSpec((tm, D), lambda i: (i, 0)),
        out_shape=jax.ShapeDtypeStruct((T, D), F32),
        compiler_params=pltpu.CompilerParams(
            dimension_semantics=("arbitrary",), vmem_limit_bytes=VMEM_LIMIT),
        name="dense_ffn",
    )(h, *consts)


def _router_kernel(h_ref, ng_ref, wr_ref, f_ref, meta_ref, route_ref, cnt_ref, carry):
    tb = h_ref.shape[0]

    @pl.when(pl.program_id(0) == 0)
    def _():
        carry[...] = jnp.zeros(carry.shape, F32)

    h = h_ref[...]
    ms = jnp.mean(h * h, axis=-1, keepdims=True)
    f = h * lax.rsqrt(ms + EPS) * ng_ref[...]
    for c in range(ROW_SUB):
        f_ref[:, c, :] = f[:, c * LANES:(c + 1) * LANES]
    f_hi, f_lo = _split_hi_lo(f)
    w_hi, w_lo = _split_hi_lo(wr_ref[...])
    logits = _dot(f_hi, w_hi) + _dot(f_lo, w_hi) + _dot(f_hi, w_lo)
    lane = lax.broadcasted_iota(jnp.int32, (tb, LANES), 1).astype(F32)
    neg = jnp.float32(-jnp.inf)
    logits = jnp.where(lane < N_EXPERTS, logits, neg)
    m1 = jnp.max(logits, axis=-1, keepdims=True)
    i1 = jnp.min(jnp.where(logits == m1, lane, float(LANES)), axis=-1, keepdims=True)
    rest = jnp.where(lane == i1, neg, logits)
    m2 = jnp.max(rest, axis=-1, keepdims=True)
    i2 = jnp.min(jnp.where(rest == m2, lane, float(LANES)), axis=-1, keepdims=True)
    e21 = jnp.exp(m2 - m1)
    g1 = 1.0 / (1.0 + e21)
    g2 = e21 * g1

    sel1 = lane == i1
    sel2 = lane == i2
    sel = jnp.where(sel1 | sel2, 1.0, 0.0)
    tri = lax.broadcasted_iota(jnp.int32, (tb, tb), 0) > lax.broadcasted_iota(jnp.int32, (tb, tb), 1)
    before = _dot(jnp.where(tri, 1.0, 0.0).astype(BF16), sel.astype(BF16)) + carry[...]
    r1 = jnp.sum(jnp.where(sel1, before, 0.0), axis=-1, keepdims=True)
    r2 = jnp.sum(jnp.where(sel2, before, 0.0), axis=-1, keepdims=True)
    blk_cnt = jnp.sum(sel, axis=0, keepdims=True)
    carry[...] = carry[...] + blk_cnt
    cnt_ref[0] = jnp.broadcast_to(blk_cnt, (8, LANES))

    vals = [i1, i2, r1, r2, g1, g2]
    meta = jnp.zeros((tb, LANES), F32)
    for j, val in enumerate(vals):
        meta = jnp.where(lane == j, val, meta)
    meta_ref[...] = meta
    route_ref[0] = meta.T[0:8, :]


def _router(h, ng, w_router, *, tb):
    T, D = h.shape
    nblk = T // tb
    wr = jnp.pad(w_router, ((0, 0), (0, LANES - N_EXPERTS)))
    return pl.pallas_call(
        _router_kernel,
        grid=(nblk,),
        in_specs=[pl.BlockSpec((tb, D), lambda i: (i, 0)), _const_spec((1, D)), _const_spec(wr.shape)],
        out_specs=[pl.BlockSpec((tb, ROW_SUB, LANES), lambda i: (i, 0, 0)),
                   pl.BlockSpec((tb, LANES), lambda i: (i, 0)),
                   pl.BlockSpec((1, 8, tb), lambda i: (i, 0, 0)),
                   pl.BlockSpec((1, 8, LANES), lambda i: (i, 0, 0))],
        out_shape=[jax.ShapeDtypeStruct((T, ROW_SUB, LANES), F32),
                   jax.ShapeDtypeStruct((T, LANES), F32),
                   jax.ShapeDtypeStruct((nblk, 8, tb), F32),
                   jax.ShapeDtypeStruct((nblk, 8, LANES), F32)],
        scratch_shapes=[pltpu.VMEM((1, LANES), F32)],
        compiler_params=pltpu.CompilerParams(
            dimension_semantics=("arbitrary",), vmem_limit_bytes=VMEM_LIMIT),
        name="router",
    )(h, ng.reshape(1, D), wr)


def _expert_kernel(te_s, nused_s, slots_s, tot_s, ends_s, f_hbm, wg_ref, wu_ref, wd_ref, out_hbm,
                   xbuf, ybuf, enc_s, sem_g, sem_s, *, n_tok, tb):
    i = pl.program_id(0)
    n_used = nused_s[0]
    tm = xbuf.shape[1]
    slot = i % 2
    unroll = 16

    def gather_copy(hbm_row, buf, r):
        return pltpu.make_async_copy(f_hbm.at[hbm_row], xbuf.at[buf, r], sem_g.at[buf])

    def scatter_copy(hbm_row, buf, r):
        return pltpu.make_async_copy(ybuf.at[buf, r], out_hbm.at[hbm_row], sem_s.at[buf])

    def start_gather(tile, buf):
        for r in range(tm):
            gather_copy(enc_s[tile * tm + r] & (n_tok - 1), buf, r).start()

    def start_scatter(tile, buf):
        for r in range(tm):
            scatter_copy(enc_s[tile * tm + r], buf, r).start()

    def wait_all(make, buf):
        for r in range(tm):
            make(0, buf, r).wait()

    @pl.when(i == 0)
    def _():
        for e in range(N_EXPERTS):
            end = ends_s[e]
            first_pad = end - (tot_s[e] + tm - 1) // tm * tm + tot_s[e]
            row0 = 2 * n_tok + ((end // tm - 1) % 2) * tm - (end - tm)

            def pad_body(j, carry, row0=row0):
                enc_s[j] = row0 + j
                return carry

            lax.fori_loop(first_pad, end, pad_body, 0)

        def fill_block(blk, carry):
            def fill_body(it, carry):
                src = blk * 2 * tb + it * unroll
                tok = blk * tb + it * unroll
                for u in range(unroll):
                    enc_s[slots_s[src + u]] = tok + u
                    enc_s[slots_s[src + tb + u]] = tok + n_tok + u
                return carry

            return lax.fori_loop(0, tb // unroll, fill_body, carry)

        lax.fori_loop(0, n_tok // tb, fill_block, 0)

        start_gather(0, 0)
        ybuf[...] = jnp.zeros(ybuf.shape, F32)
        fills = [pltpu.make_async_copy(ybuf.at[b], out_hbm.at[pl.ds(2 * n_tok + b * tm, tm)], sem_s.at[b])
                 for b in range(2)]
        for cp in fills:
            cp.start()
        for cp in fills:
            cp.wait()

    @pl.when(i < n_used)
    def _():
        wait_all(gather_copy, slot)

        start_gather(jnp.minimum(i + 1, n_used - 1), 1 - slot)

        x = jnp.concatenate([xbuf[slot, :, c, :] for c in range(ROW_SUB)], axis=-1).astype(BF16)
        y = _swiglu_rows(x, wg_ref, wu_ref, wd_ref, lead=(0,))

        @pl.when(i >= 2)
        def _():
            wait_all(scatter_copy, slot)

        for c in range(ROW_SUB):
            ybuf[slot, :, c, :] = y[:, c * LANES:(c + 1) * LANES]
        start_scatter(i, slot)

        @pl.when(i == n_used - 1)
        def _():
            wait_all(gather_copy, 1 - slot)
            wait_all(scatter_copy, slot)

            @pl.when(n_used >= 2)
            def _():
                wait_all(scatter_copy, 1 - slot)


def _experts(f3, tile_expert, n_used, slots, tot, ends, w_g, w_u, w_d, *, tm, tb):
    n_tok = f3.shape[0]
    n_tiles = tile_expert.shape[0]
    D = w_g.shape[1]
    d_ff = w_g.shape[-1]
    grid_spec = pltpu.PrefetchScalarGridSpec(
        num_scalar_prefetch=5,
        grid=(n_tiles,),
        in_specs=[pl.BlockSpec(memory_space=pl.ANY),
                  pl.BlockSpec((1, D, d_ff), lambda i, te, *_: (te[i], 0, 0)),
                  pl.BlockSpec((1, D, d_ff), lambda i, te, *_: (te[i], 0, 0)),
                  pl.BlockSpec((1, d_ff, D), lambda i, te, *_: (te[i], 0, 0))],
        out_specs=pl.BlockSpec(memory_space=pl.ANY),
        scratch_shapes=[pltpu.VMEM((2, tm, ROW_SUB, LANES), F32),
                        pltpu.VMEM((2, tm, ROW_SUB, LANES), F32),
                        pltpu.SMEM((n_tiles * tm,), jnp.int32),
                        pltpu.SemaphoreType.DMA((2,)),
                        pltpu.SemaphoreType.DMA((2,))],
    )
    return pl.pallas_call(
        functools.partial(_expert_kernel, n_tok=n_tok, tb=tb),
        grid_spec=grid_spec,
        out_shape=jax.ShapeDtypeStruct((2 * n_tok + 2 * tm, ROW_SUB, LANES), F32),
        compiler_params=pltpu.CompilerParams(
            dimension_semantics=("arbitrary",), vmem_limit_bytes=VMEM_LIMIT),
        name="experts",
    )(tile_expert, n_used, slots, tot, ends, f3, w_g, w_u, w_d)


def _final_kernel(h_ref, meta_ref, y1_ref, y2_ref, ng_ref, o_ref):
    g1 = meta_ref[:, 4:5]
    g2 = meta_ref[:, 5:6]
    moe = jnp.concatenate([g1 * y1_ref[:, c, :] + g2 * y2_ref[:, c, :] for c in range(ROW_SUB)], axis=-1)
    hh = h_ref[...] + moe
    ms = jnp.mean(hh * hh, axis=-1, keepdims=True)
    o_ref[...] = hh * lax.rsqrt(ms + EPS) * ng_ref[...]


def _final(h, meta, y3, ng, *, tb):
    T, D = h.shape
    nblk = T // tb
    return pl.pallas_call(
        _final_kernel,
        grid=(nblk,),
        in_specs=[pl.BlockSpec((tb, D), lambda i: (i, 0)),
                  pl.BlockSpec((tb, LANES), lambda i: (i, 0)),
                  pl.BlockSpec((tb, ROW_SUB, LANES), lambda i: (i, 0, 0)),
                  pl.BlockSpec((tb, ROW_SUB, LANES), lambda i: (i + nblk, 0, 0)),
                  _const_spec((1, D))],
        out_specs=pl.BlockSpec((tb, D), lambda i: (i, 0)),
        out_shape=jax.ShapeDtypeStruct((T, D), F32),
        compiler_params=pltpu.CompilerParams(
            dimension_semantics=("arbitrary",), vmem_limit_bytes=VMEM_LIMIT),
        name="final",
    )(h, meta, y3, y3, ng.reshape(1, D))


def _moe_and_final_norm(h, ng, w_router, w_g, w_u, w_d, final_g, *, tm, tb):
    T, D = h.shape
    tb = min(tb, T)
    tm = min(tm, tb)
    E = N_EXPERTS
    assert D == ROW_SUB * LANES and T % tb == 0 and T & (T - 1) == 0
    n_tiles = (2 * T) // tm + E
    f3, meta, route, blk_cnt = _router(h, ng, w_router, tb=tb)

    tot = jnp.sum(blk_cnt[:, 0, :E], axis=0).astype(jnp.int32)
    padded = (tot + tm - 1) // tm * tm
    ends = jnp.cumsum(padded)
    off = ends - padded
    route = route.astype(jnp.int32)
    e12 = route[:, 0:2, :]
    slots = route[:, 2:4, :]
    for e in range(E):
        slots = slots + jnp.where(e12 == e, off[e], 0)
    tile_start = jnp.arange(n_tiles, dtype=jnp.int32) * tm
    tile_expert = jnp.minimum(jnp.sum(ends[None, :] <= tile_start[:, None], axis=1), E - 1).astype(jnp.int32)
    n_used = (ends[-1] // tm).reshape(1)

    y3 = _experts(f3, tile_expert, n_used, slots.reshape(-1), tot, ends,
                  w_g.astype(BF16), w_u.astype(BF16), w_d.astype(BF16), tm=tm, tb=tb)
    return _final(h, meta, y3, final_g, tb=tb)


def kernel(x, attn_norm_g, w_in, sc_conv_w, cf_conv_w, cf_conv_b, cf_ln_g, cf_ln_b, gla_w_a2, gla_b_a,
           gla_norm_g, w_out, ffn_norm_g, dense_w_gate, dense_w_up, dense_w_down, moe_w_router,
           moe_w_gate, moe_w_up, moe_w_down, final_norm_g):
    B, S, D = x.shape
    depth = w_in.shape[0]
    assert depth == 2, "layer schedule below is dense FFN then routed experts"
    h = x
    cast_jobs = [[moe_w_gate[0], moe_w_up[0]], [moe_w_down[0]]]
    cast = []
    for layer in range(depth):
        h, done = _mixer(h, attn_norm_g[layer], w_in[layer], sc_conv_w[layer], cf_conv_w[layer],
                         cf_conv_b[layer], cf_ln_g[layer], cf_ln_b[layer], gla_w_a2[layer], gla_b_a[layer],
                         gla_norm_g[layer], w_out[layer], ts=512, cast_jobs=cast_jobs[layer])
        cast += done
        h2 = h.reshape(B * S, D)
        if layer % 2 == 0:
            h = _dense_ffn(h2, ffn_norm_g[layer], dense_w_gate[0], dense_w_up[0], dense_w_down[0],
                           tm=512).reshape(B, S, D)
        else:
            h = _moe_and_final_norm(h2, ffn_norm_g[layer], moe_w_router[0], *cast, final_norm_g,
                                    tm=512, tb=512).reshape(B, S, D)
    return h
```

```python
import functools

import numpy as np
import jax
import jax.numpy as jnp
from jax import lax
from jax.experimental import pallas as pl
from jax.experimental.pallas import tpu as pltpu

F32 = jnp.float32
BF16 = jnp.bfloat16

EPS = 1e-6
SC_K = 3
CF_K = 31
GLA_H = 4
GLA_TAU = 16.0
CHUNK = 64
GATE_STEP_BOUND = 1.0
N_EXPERTS = 8
LANES = 128
ROW_SUB = 8
VMEM_LIMIT = 56 * 1024 * 1024

W = 256
C_SCB, C_SCC, C_SCV, C_CFA, C_CFG, C_Q, C_K = (i * W for i in range(7))
C_V = 7 * W
C_R = C_V + 2 * W
C_ALR = C_R + 2 * W
D_INP = C_ALR + LANES


def _split_hi_lo(x):
    hi = x.astype(BF16)
    lo = (x - hi.astype(F32)).astype(BF16)
    return hi, lo


def _dot(a, b):
    return jnp.dot(a, b, preferred_element_type=F32)


def _dot_nt(a, b):
    return lax.dot_general(a, b, (((1,), (1,)), ((), ())), preferred_element_type=F32)


def _dot_tn(a, b):
    return lax.dot_general(a, b, (((0,), (0,)), ((), ())), preferred_element_type=F32)


def _level_matrices():
    mats = [np.tril(np.ones((CHUNK, CHUNK), np.float32))]
    s = CHUNK // 2
    while s >= 1:
        m = np.zeros((CHUNK, CHUNK), np.float32)
        for i in range(CHUNK):
            mid = (i // (2 * s)) * 2 * s + s
            if i >= mid:
                m[i, mid:i + 1] = 1.0
            else:
                m[i, i + 1:mid] = 1.0
        mats.append(m)
        s //= 2
    return np.concatenate(mats, axis=0)


N_LEVELS = int(np.log2(CHUNK))


def _gla_chunk(c, p_scr, g_scr, y_scr, st_scr, gng, mall, *, bounded):
    dk = W // GLA_H
    dv = 2 * W // GLA_H
    if isinstance(c, int):
        rows = slice(c * CHUNK, (c + 1) * CHUNK)
    else:
        rows = pl.ds(pl.multiple_of(c * CHUNK, CHUNK), CHUNK)
    q = p_scr[rows, C_Q:C_Q + W] * (dk ** -0.5)
    k = p_scr[rows, C_K:C_K + W]
    v = p_scr[rows, C_V:C_V + 2 * W].astype(BF16)
    r = p_scr[rows, C_R:C_R + 2 * W]
    g_hi, g_lo = _split_hi_lo(g_scr[rows, :])

    dk_sh = dk.bit_length() - 1
    dv_sh = dv.bit_length() - 1
    row = lax.broadcasted_iota(jnp.int32, (CHUNK, W), 0)
    lane_head = lax.broadcasted_iota(jnp.int32, (CHUNK, W), 1) >> dk_sh
    srow = lax.broadcasted_iota(jnp.int32, (GLA_H * CHUNK, CHUNK), 0) & (CHUNK - 1)
    scol = lax.broadcasted_iota(jnp.int32, (GLA_H * CHUNK, CHUNK), 1)

    def stack_heads(x):
        xb = x.astype(BF16)
        zero = jnp.zeros_like(xb)
        return jnp.concatenate([jnp.where(lane_head == h, xb, zero) for h in range(GLA_H)], axis=0)

    if bounded:
        cum = mall[0:CHUNK]
        b = _dot(cum, g_hi) + _dot(cum, g_lo)
        ql = q * jnp.exp(b)
        kl = k * jnp.exp(-b)
        scores = jnp.where(srow >= scol, _dot_nt(stack_heads(ql), kl.astype(BF16)), 0.0)
    else:
        ex = _dot(mall, g_hi) + _dot(mall, g_lo)
        b = ex[0:CHUNK]
        scores = jnp.where(srow == scol, _dot_nt(stack_heads(q), k.astype(BF16)), 0.0)
        for lvl in range(1, N_LEVELS + 1):
            s = CHUNK >> lvl
            sh = s.bit_length() - 1
            e = jnp.exp(ex[lvl * CHUNK:(lvl + 1) * CHUNK])
            second = ((row >> sh) & 1) == 1
            ql = jnp.where(second, q * e, 0.0)
            kl = jnp.where(second, 0.0, k * e)
            same = (srow >> (sh + 1)) == (scol >> (sh + 1))
            scores = scores + jnp.where(same, _dot_nt(stack_heads(ql), kl.astype(BF16)), 0.0)
    b_last = b[CHUNK - 1:CHUNK]

    vhead = lax.broadcasted_iota(jnp.int32, (CHUNK, 2 * W), 1) >> dv_sh
    o_stack = _dot(scores.astype(BF16), v)
    q_in = (q * jnp.exp(b)).astype(BF16)
    st = st_scr[...]
    o = _dot_nt(q_in, st.astype(BF16))
    for h in range(GLA_H):
        o = o + jnp.where(vhead == h, o_stack[h * CHUNK:(h + 1) * CHUNK], 0.0)

    k_s = (k * jnp.exp(b_last - b)).astype(BF16)
    upd = _dot_tn(v, k_s)
    rhead = lax.broadcasted_iota(jnp.int32, (2 * W, W), 0) >> dv_sh
    chead = lax.broadcasted_iota(jnp.int32, (2 * W, W), 1) >> dk_sh
    st_scr[...] = st * jnp.exp(b_last) + jnp.where(rhead == chead, upd, 0.0)

    for h in range(GLA_H):
        oh = o[:, h * dv:(h + 1) * dv]
        ms = jnp.mean(oh * oh, axis=-1, keepdims=True)
        rh = r[:, h * dv:(h + 1) * dv]
        yh = oh * lax.rsqrt(ms + EPS) * gng[:, h * dv:(h + 1) * dv] * (rh * jax.nn.sigmoid(rh))
        y_scr[rows, 2 * W + h * dv:2 * W + (h + 1) * dv] = yh.astype(BF16)


def _mixer_kernel(*refs, ts, n_cast):
    (x_ref, ng_ref, win_ref, scw_ref, cfw_ref, cfb_ref, lng_ref, lnb_ref,
     wa2_ref, ba_ref, gng_ref, mall_ref, wout_ref) = refs[:13]
    cast_in = refs[13:13 + n_cast]
    o_ref = refs[13 + n_cast]
    cast_out = refs[14 + n_cast:14 + 2 * n_cast]
    p_scr, sc_buf, cf_buf, st_scr, y_scr, g_scr = refs[14 + 2 * n_cast:]
    sc_halo = 8
    cf_halo = 32

    for src, dst in zip(cast_in, cast_out):
        dst[...] = src[...].astype(BF16)

    @pl.when(pl.program_id(1) == 0)
    def _():
        sc_buf[0:sc_halo, :] = jnp.zeros((sc_halo, W), F32)
        cf_buf[0:cf_halo, :] = jnp.zeros((cf_halo, W), F32)
        st_scr[...] = jnp.zeros(st_scr.shape, F32)

    x = x_ref[0]
    ms = jnp.mean(x * x, axis=-1, keepdims=True)
    a = (x * lax.rsqrt(ms + EPS) * ng_ref[...]).astype(BF16)
    p_scr[...] = _dot(a, win_ref[...])

    sc_buf[sc_halo:sc_halo + ts, :] = p_scr[:, C_SCC:C_SCC + W] * p_scr[:, C_SCV:C_SCV + W]
    cf_buf[cf_halo:cf_halo + ts, :] = p_scr[:, C_CFA:C_CFA + W] * jax.nn.sigmoid(p_scr[:, C_CFG:C_CFG + W])
    scw = scw_ref[...]
    cfw = cfw_ref[...]
    for c in range(ts // CHUNK):
        r0 = c * CHUNK
        conv = jnp.zeros((CHUNK, W), F32)
        for j in range(SC_K):
            o0 = r0 + sc_halo - (SC_K - 1) + j
            conv = conv + sc_buf[o0:o0 + CHUNK, :] * scw[j:j + 1]
        y_scr[r0:r0 + CHUNK, 0:W] = (p_scr[r0:r0 + CHUNK, C_SCB:C_SCB + W] * conv).astype(BF16)

        u = jnp.zeros((CHUNK, W), F32) + cfb_ref[...]
        tap0 = cf_halo - (CF_K - 1)
        for rr in range(ROW_SUB):
            n_rows = CHUNK if rr == 0 else CHUNK + ROW_SUB
            part = None
            for j in range(CF_K):
                if (tap0 + j) % ROW_SUB != rr:
                    continue
                a0 = r0 + (tap0 + j) // ROW_SUB * ROW_SUB
                term = cf_buf[a0:a0 + n_rows, :] * cfw[j:j + 1]
                part = term if part is None else part + term
            u = u + part[rr:rr + CHUNK]
        mu = jnp.mean(u, axis=-1, keepdims=True)
        d = u - mu
        var = jnp.mean(d * d, axis=-1, keepdims=True)
        yn = d * lax.rsqrt(var + EPS) * lng_ref[...] + lnb_ref[...]
        y_scr[r0:r0 + CHUNK, W:2 * W] = (yn * jax.nn.sigmoid(yn)).astype(BF16)
    sc_buf[0:sc_halo, :] = sc_buf[ts:ts + sc_halo, :]
    cf_buf[0:cf_halo, :] = cf_buf[ts:ts + cf_halo, :]

    wa2_hi, wa2_lo = _split_hi_lo(wa2_ref[...])
    alr_hi, alr_lo = _split_hi_lo(p_scr[:, C_ALR:C_ALR + LANES])
    z = _dot(alr_hi, wa2_hi) + _dot(alr_lo, wa2_hi) + _dot(alr_hi, wa2_lo) + ba_ref[...]
    g_scr[...] = (jnp.minimum(z, 0.0) - jnp.log1p(jnp.exp(-jnp.abs(z)))) * (1.0 / GLA_TAU)
    gng = gng_ref[...]
    mall = mall_ref[...]
    bounded = jnp.min(g_scr[...]) >= -GATE_STEP_BOUND

    @pl.when(bounded)
    def _():
        for c in range(ts // CHUNK):
            _gla_chunk(c, p_scr, g_scr, y_scr, st_scr, gng, mall, bounded=True)

    @pl.when(jnp.logical_not(bounded))
    def _():
        def body(c, carry):
            _gla_chunk(c, p_scr, g_scr, y_scr, st_scr, gng, mall, bounded=False)
            return carry
        lax.fori_loop(0, ts // CHUNK, body, 0)

    o_ref[0] = x + _dot(y_scr[...], wout_ref[...])


def _const_spec(shape):
    nd = len(shape)
    return pl.BlockSpec(shape, lambda *_: (0,) * nd)


def _mixer(x, ng, w_in, sc_w, cf_w, cf_b, ln_g, ln_b, w_a2, b_a, gn_g, w_out, *, ts, cast_jobs=()):
    B, S, D = x.shape
    ts = min(ts, S)
    n_steps = B * (S // ts)
    nt = S // ts
    cast_views = []
    for w in cast_jobs:
        rows = int(np.prod(w.shape[:-1]))
        assert rows % (n_steps * 16) == 0, "cast slice must be whole bf16 row tiles"
        cast_views.append(w.reshape(n_steps, rows // n_steps, w.shape[-1]))
    split = np.cumsum([0, W, W, W, W, W, W, W, 2 * W, 16, 2 * W])
    cols = [w_in[:, split[i]:split[i + 1]] for i in range(10)]
    w_alr = jnp.pad(cols[8], ((0, 0), (0, LANES - 16)))
    w_in_r = jnp.concatenate(cols[:8] + [cols[9], w_alr], axis=1).astype(BF16)
    scw = jnp.pad(sc_w, ((0, 8 - SC_K), (0, 0)))
    cfw = jnp.pad(cf_w, ((0, 32 - CF_K), (0, 0)))
    wa2 = jnp.pad(w_a2, ((0, LANES - 16), (0, 0)))
    mall = jnp.asarray(_level_matrices(), BF16)
    consts = [ng.reshape(1, D), w_in_r, scw, cfw, cf_b.reshape(1, W), ln_g.reshape(1, W),
              ln_b.reshape(1, W), wa2, b_a.reshape(1, W), gn_g.reshape(1, 2 * W), mall,
              w_out.astype(BF16)]
    cast_specs = [pl.BlockSpec((1,) + v.shape[1:], lambda b, t: (b * nt + t, 0, 0)) for v in cast_views]
    outs = pl.pallas_call(
        functools.partial(_mixer_kernel, ts=ts, n_cast=len(cast_views)),
        grid=(B, nt),
        in_specs=([pl.BlockSpec((1, ts, D), lambda b, t: (b, t, 0))] + [_const_spec(c.shape) for c in consts]
                  + cast_specs),
        out_specs=[pl.BlockSpec((1, ts, D), lambda b, t: (b, t, 0))] + cast_specs,
        out_shape=[jax.ShapeDtypeStruct((B, S, D), F32)]
        + [jax.ShapeDtypeStruct(v.shape, BF16) for v in cast_views],
        scratch_shapes=[
            pltpu.VMEM((ts, D_INP), F32),
            pltpu.VMEM((ts + 8, W), F32),
            pltpu.VMEM((ts + 32, W), F32),
            pltpu.VMEM((2 * W, W), F32),
            pltpu.VMEM((ts, D), BF16),
            pltpu.VMEM((ts, W), F32),
        ],
        compiler_params=pltpu.CompilerParams(
            dimension_semantics=("arbitrary", "arbitrary"), vmem_limit_bytes=VMEM_LIMIT),
        name="mixer",
    )(x, *consts, *cast_views)
    return outs[0], [o.reshape(w.shape) for o, w in zip(outs[1:], cast_jobs)]


def _ff_chunks(d_ff, step=1024):
    return [(s, min(s + step, d_ff)) for s in range(0, d_ff, step)]


def _swiglu_rows(f, wg_ref, wu_ref, wd_ref, act_scr, lead=()):
    d_ff = wg_ref.shape[-1]
    for s, e in _ff_chunks(d_ff):
        gate = _dot(f, wg_ref[lead + (slice(None), slice(s, e))])
        up = _dot(f, wu_ref[lead + (slice(None), slice(s, e))])
        act_scr[:, s:e] = (gate * jax.nn.sigmoid(gate) * up).astype(BF16)
    return _dot(act_scr[...], wd_ref[lead + (slice(None), slice(None))])


def _ffn_kernel(h_ref, ng_ref, wg_ref, wu_ref, wd_ref, o_ref, act_scr):
    h = h_ref[...]
    ms = jnp.mean(h * h, axis=-1, keepdims=True)
    f = (h * lax.rsqrt(ms + EPS) * ng_ref[...]).astype(BF16)
    o_ref[...] = h + _swiglu_rows(f, wg_ref, wu_ref, wd_ref, act_scr)


def _dense_ffn(h, ng, w_g, w_u, w_d, *, tm):
    T, D = h.shape
    tm = min(tm, T)
    consts = [ng.reshape(1, D), w_g.astype(BF16), w_u.astype(BF16), w_d.astype(BF16)]
    return pl.pallas_call(
        _ffn_kernel,
        grid=(T // tm,),
        in_specs=[pl.BlockSpec((tm, D), lambda i: (i, 0))] + [_const_spec(c.shape) for c in consts],
        out_specs=pl.BlockSpec((tm, D), lambda i: (i, 0)),
        out_shape=jax.ShapeDtypeStruct((T, D), F32),
        scratch_shapes=[pltpu.VMEM((tm, w_g.shape[-1]), BF16)],
        compiler_params=pltpu.CompilerParams(
            dimension_semantics=("arbitrary",), vmem_limit_bytes=VMEM_LIMIT),
        name="dense_ffn",
    )(h, *consts)


def _router_kernel(h_ref, ng_ref, wr_ref, f_ref, meta_ref, route_ref, cnt_ref, carry):
    tb = h_ref.shape[0]

    @pl.when(pl.program_id(0) == 0)
    def _():
        carry[...] = jnp.zeros(carry.shape, F32)

    h = h_ref[...]
    ms = jnp.mean(h * h, axis=-1, keepdims=True)
    f = h * lax.rsqrt(ms + EPS) * ng_ref[...]
    for c in range(ROW_SUB):
        f_ref[pl.ds(c, tb, stride=ROW_SUB), :] = f[:, c * LANES:(c + 1) * LANES]
    f_hi, f_lo = _split_hi_lo(f)
    w_hi, w_lo = _split_hi_lo(wr_ref[...])
    logits = _dot(f_hi, w_hi) + _dot(f_lo, w_hi) + _dot(f_hi, w_lo)
    lane = lax.broadcasted_iota(jnp.int32, (tb, LANES), 1).astype(F32)
    neg = jnp.float32(-jnp.inf)
    logits = jnp.where(lane < N_EXPERTS, logits, neg)
    m1 = jnp.max(logits, axis=-1, keepdims=True)
    i1 = jnp.min(jnp.where(logits == m1, lane, float(LANES)), axis=-1, keepdims=True)
    rest = jnp.where(lane == i1, neg, logits)
    m2 = jnp.max(rest, axis=-1, keepdims=True)
    i2 = jnp.min(jnp.where(rest == m2, lane, float(LANES)), axis=-1, keepdims=True)
    e21 = jnp.exp(m2 - m1)
    g1 = 1.0 / (1.0 + e21)
    g2 = e21 * g1

    sel1 = lane == i1
    sel2 = lane == i2
    sel = jnp.where(sel1 | sel2, 1.0, 0.0)
    tri = lax.broadcasted_iota(jnp.int32, (tb, tb), 0) > lax.broadcasted_iota(jnp.int32, (tb, tb), 1)
    before = _dot(jnp.where(tri, 1.0, 0.0).astype(BF16), sel.astype(BF16)) + carry[...]
    r1 = jnp.sum(jnp.where(sel1, before, 0.0), axis=-1, keepdims=True)
    r2 = jnp.sum(jnp.where(sel2, before, 0.0), axis=-1, keepdims=True)
    blk_cnt = jnp.sum(sel, axis=0, keepdims=True)
    carry[...] = carry[...] + blk_cnt
    cnt_ref[0] = jnp.broadcast_to(blk_cnt, (8, LANES))

    vals = [i1, i2, r1, r2, g1, g2]
    meta = jnp.zeros((tb, LANES), F32)
    for j, val in enumerate(vals):
        meta = jnp.where(lane == j, val, meta)
    meta_ref[...] = meta
    route_ref[0] = meta.T[0:8, :]


def _router(h, ng, w_router, *, tb):
    T, D = h.shape
    nblk = T // tb
    wr = jnp.pad(w_router, ((0, 0), (0, LANES - N_EXPERTS)))
    return pl.pallas_call(
        _router_kernel,
        grid=(nblk,),
        in_specs=[pl.BlockSpec((tb, D), lambda i: (i, 0)), _const_spec((1, D)), _const_spec(wr.shape)],
        out_specs=[pl.BlockSpec((tb * ROW_SUB, LANES), lambda i: (i, 0)),
                   pl.BlockSpec((tb, LANES), lambda i: (i, 0)),
                   pl.BlockSpec((1, 8, tb), lambda i: (i, 0, 0)),
                   pl.BlockSpec((1, 8, LANES), lambda i: (i, 0, 0))],
        out_shape=[jax.ShapeDtypeStruct((T * ROW_SUB, LANES), F32),
                   jax.ShapeDtypeStruct((T, LANES), F32),
                   jax.ShapeDtypeStruct((nblk, 8, tb), F32),
                   jax.ShapeDtypeStruct((nblk, 8, LANES), F32)],
        scratch_shapes=[pltpu.VMEM((1, LANES), F32)],
        compiler_params=pltpu.CompilerParams(
            dimension_semantics=("arbitrary",), vmem_limit_bytes=VMEM_LIMIT),
        name="router",
    )(h, ng.reshape(1, D), wr)


def _expert_kernel(te_s, nused_s, slots_s, tot_s, ends_s, f_hbm, wg_ref, wu_ref, wd_ref, out_hbm,
                   xbuf, ybuf, act_scr, enc_s, sem_g, sem_s, *, n_tok, tb):
    i = pl.program_id(0)
    n_used = nused_s[0]
    tm = xbuf.shape[1] // ROW_SUB
    slot = i % 2
    unroll = 16

    def hbm_rows(token_row):
        return pl.ds(pl.multiple_of(token_row * ROW_SUB, ROW_SUB), ROW_SUB)

    def gather_copy(hbm_row, buf, r):
        return pltpu.make_async_copy(f_hbm.at[hbm_rows(hbm_row)], xbuf.at[buf, pl.ds(r * ROW_SUB, ROW_SUB)],
                                     sem_g.at[buf])

    def scatter_copy(hbm_row, buf, r):
        return pltpu.make_async_copy(ybuf.at[buf, pl.ds(r * ROW_SUB, ROW_SUB)], out_hbm.at[hbm_rows(hbm_row)],
                                     sem_s.at[buf])

    def start_gather(tile, buf):
        for r in range(tm):
            gather_copy(enc_s[tile * tm + r] & (n_tok - 1), buf, r).start()

    def start_scatter(tile, buf):
        for r in range(tm):
            scatter_copy(enc_s[tile * tm + r], buf, r).start()

    def wait_all(make, buf):
        for r in range(tm):
            make(0, buf, r).wait()

    @pl.when(i == 0)
    def _():
        for e in range(N_EXPERTS):
            end = ends_s[e]
            first_pad = end - (tot_s[e] + tm - 1) // tm * tm + tot_s[e]
            row0 = 2 * n_tok + ((end // tm - 1) % 2) * tm - (end - tm)

            def pad_body(j, carry, row0=row0):
                enc_s[j] = row0 + j
                return carry

            lax.fori_loop(first_pad, end, pad_body, 0)

        def fill_block(blk, carry):
            def fill_body(it, carry):
                src = blk * 2 * tb + it * unroll
                tok = blk * tb + it * unroll
                for u in range(unroll):
                    enc_s[slots_s[src + u]] = tok + u
                    enc_s[slots_s[src + tb + u]] = tok + n_tok + u
                return carry

            return lax.fori_loop(0, tb // unroll, fill_body, carry)

        lax.fori_loop(0, n_tok // tb, fill_block, 0)

        start_gather(0, 0)
        ybuf[...] = jnp.zeros(ybuf.shape, F32)
        fills = [pltpu.make_async_copy(ybuf.at[b], out_hbm.at[pl.ds((2 * n_tok + b * tm) * ROW_SUB, tm * ROW_SUB)],
                                       sem_s.at[b]) for b in range(2)]
        for cp in fills:
            cp.start()
        for cp in fills:
            cp.wait()

    def tile_step(buf):
        wait_all(gather_copy, buf)

        start_gather(jnp.minimum(i + 1, n_used - 1), 1 - buf)

        x = jnp.concatenate([xbuf[buf, pl.ds(c, tm, stride=ROW_SUB), :] for c in range(ROW_SUB)],
                            axis=-1).astype(BF16)
        y = _swiglu_rows(x, wg_ref, wu_ref, wd_ref, act_scr, lead=(0,))

        @pl.when(i >= 2)
        def _():
            wait_all(scatter_copy, buf)

        for c in range(ROW_SUB):
            ybuf[buf, pl.ds(c, tm, stride=ROW_SUB), :] = y[:, c * LANES:(c + 1) * LANES]
        start_scatter(i, buf)

        @pl.when(i == n_used - 1)
        def _():
            wait_all(gather_copy, 1 - buf)
            wait_all(scatter_copy, buf)

            @pl.when(n_used >= 2)
            def _():
                wait_all(scatter_copy, 1 - buf)

    for buf in range(2):
        pl.when(jnp.logical_and(i < n_used, slot == buf))(functools.partial(tile_step, buf))


def _experts(f3, tile_expert, n_used, slots, tot, ends, w_g, w_u, w_d, *, tm, tb):
    n_tok = f3.shape[0] // ROW_SUB
    n_tiles = tile_expert.shape[0]
    D = w_g.shape[1]
    d_ff = w_g.shape[-1]
    grid_spec = pltpu.PrefetchScalarGridSpec(
        num_scalar_prefetch=5,
        grid=(n_tiles,),
        in_specs=[pl.BlockSpec(memory_space=pl.ANY),
                  pl.BlockSpec((1, D, d_ff), lambda i, te, *_: (te[i], 0, 0)),
                  pl.BlockSpec((1, D, d_ff), lambda i, te, *_: (te[i], 0, 0)),
                  pl.BlockSpec((1, d_ff, D), lambda i, te, *_: (te[i], 0, 0))],
        out_specs=pl.BlockSpec(memory_space=pl.ANY),
        scratch_shapes=[pltpu.VMEM((2, tm * ROW_SUB, LANES), F32),
                        pltpu.VMEM((2, tm * ROW_SUB, LANES), F32),
                        pltpu.VMEM((tm, d_ff), BF16),
                        pltpu.SMEM((n_tiles * tm,), jnp.int32),
                        pltpu.SemaphoreType.DMA((2,)),
                        pltpu.SemaphoreType.DMA((2,))],
    )
    return pl.pallas_call(
        functools.partial(_expert_kernel, n_tok=n_tok, tb=tb),
        grid_spec=grid_spec,
        out_shape=jax.ShapeDtypeStruct(((2 * n_tok + 2 * tm) * ROW_SUB, LANES), F32),
        compiler_params=pltpu.CompilerParams(
            dimension_semantics=("arbitrary",), vmem_limit_bytes=VMEM_LIMIT),
        name="experts",
    )(tile_expert, n_used, slots, tot, ends, f3, w_g, w_u, w_d)


def _final_kernel(h_ref, meta_ref, y1_ref, y2_ref, ng_ref, o_ref):
    g1 = meta_ref[:, 4:5]
    g2 = meta_ref[:, 5:6]
    tb = h_ref.shape[0]
    moe = jnp.concatenate([g1 * y1_ref[pl.ds(c, tb, stride=ROW_SUB), :] + g2 * y2_ref[pl.ds(c, tb, stride=ROW_SUB), :]
                           for c in range(ROW_SUB)], axis=-1)
    hh = h_ref[...] + moe
    ms = jnp.mean(hh * hh, axis=-1, keepdims=True)
    o_ref[...] = hh * lax.rsqrt(ms + EPS) * ng_ref[...]


def _final(h, meta, y3, ng, *, tb):
    T, D = h.shape
    nblk = T // tb
    return pl.pallas_call(
        _final_kernel,
        grid=(nblk,),
        in_specs=[pl.BlockSpec((tb, D), lambda i: (i, 0)),
                  pl.BlockSpec((tb, LANES), lambda i: (i, 0)),
                  pl.BlockSpec((tb * ROW_SUB, LANES), lambda i: (i, 0)),
                  pl.BlockSpec((tb * ROW_SUB, LANES), lambda i: (i + nblk, 0)),
                  _const_spec((1, D))],
        out_specs=pl.BlockSpec((tb, D), lambda i: (i, 0)),
        out_shape=jax.ShapeDtypeStruct((T, D), F32),
        compiler_params=pltpu.CompilerParams(
            dimension_semantics=("arbitrary",), vmem_limit_bytes=VMEM_LIMIT),
        name="final",
    )(h, meta, y3, y3, ng.reshape(1, D))


def _moe_and_final_norm(h, ng, w_router, w_g, w_u, w_d, final_g, *, tm, tb):
    T, D = h.shape
    tb = min(tb, T)
    tm = min(tm, tb)
    E = N_EXPERTS
    assert D == ROW_SUB * LANES and T % tb == 0 and T & (T - 1) == 0
    n_tiles = (2 * T) // tm + E
    f3, meta, route, blk_cnt = _router(h, ng, w_router, tb=tb)

    tot = jnp.sum(blk_cnt[:, 0, :E], axis=0).astype(jnp.int32)
    padded = (tot + tm - 1) // tm * tm
    ends = jnp.cumsum(padded)
    off = ends - padded
    route = route.astype(jnp.int32)
    e12 = route[:, 0:2, :]
    slots = route[:, 2:4, :]
    for e in range(E):
        slots = slots + jnp.where(e12 == e, off[e], 0)
    tile_start = jnp.arange(n_tiles, dtype=jnp.int32) * tm
    tile_expert = jnp.minimum(jnp.sum(ends[None, :] <= tile_start[:, None], axis=1), E - 1).astype(jnp.int32)
    n_used = (ends[-1] // tm).reshape(1)

    y3 = _experts(f3, tile_expert, n_used, slots.reshape(-1), tot, ends,
                  w_g.astype(BF16), w_u.astype(BF16), w_d.astype(BF16), tm=tm, tb=tb)
    return _final(h, meta, y3, final_g, tb=tb)


def kernel(x, attn_norm_g, w_in, sc_conv_w, cf_conv_w, cf_conv_b, cf_ln_g, cf_ln_b, gla_w_a2, gla_b_a,
           gla_norm_g, w_out, ffn_norm_g, dense_w_gate, dense_w_up, dense_w_down, moe_w_router,
           moe_w_gate, moe_w_up, moe_w_down, final_norm_g):
    B, S, D = x.shape
    depth = w_in.shape[0]
    assert depth == 2, "layer schedule below is dense FFN then routed experts"
    h = x
    cast_jobs = [[moe_w_gate[0], moe_w_up[0]], [moe_w_down[0]]]
    cast = []
    for layer in range(depth):
        h, done = _mixer(h, attn_norm_g[layer], w_in[layer], sc_conv_w[layer], cf_conv_w[layer],
                         cf_conv_b[layer], cf_ln_g[layer], cf_ln_b[layer], gla_w_a2[layer], gla_b_a[layer],
                         gla_norm_g[layer], w_out[layer], ts=512, cast_jobs=cast_jobs[layer])
        cast += done
        h2 = h.reshape(B * S, D)
        if layer % 2 == 0:
            h = _dense_ffn(h2, ffn_norm_g[layer], dense_w_gate[0], dense_w_up[0], dense_w_down[0],
                           tm=512).reshape(B, S, D)
        else:
            h = _moe_and_final_norm(h2, ffn_norm_g[layer], moe_w_router[0], *cast, final_norm_g,
                                    tm=512, tb=512).reshape(B, S, D)
    return h
```
